```python
import jax
import jax.numpy as jnp
from jax import lax
import numpy as np


D_MODEL = 2048
BATCH = 4
SEQ = 4096
DEPTH = 2

CONV_DIM = 1536
CONV_WIDTH = 31
ATT_HEAD_DIM = 128
ATT_HEADS_PER_GROUP = 4
ATT_PATTERNS = ((128, 1), (512, 4), (2048, 16))
ATT_GROUPS = 3
ATT_HEADS = ATT_GROUPS * ATT_HEADS_PER_GROUP
ATT_OUT_DIM = ATT_HEADS_PER_GROUP * ATT_HEAD_DIM
ATT_BLOCK = 128
RET_HEADS = 8
RET_QK_DIM = 128
RET_V_DIM = 256
RET_CHUNK = 128
N_BRANCHES = 3
D_FF = 5632
N_EXPERTS = 8
TOP_K = 2
D_EXPERT = 7168
N_DENSE = (DEPTH + 1) // 2
N_MOE = DEPTH // 2
EPS = 1e-6
COLS_A = 2 * CONV_DIM
COLS_B = 3 * ATT_HEADS * ATT_HEAD_DIM
COLS_QK = RET_HEADS * RET_QK_DIM
COLS_V = RET_HEADS * RET_V_DIM
COLS_GATE = N_BRANCHES * D_MODEL
IN_SPLITS = (COLS_A, COLS_B, COLS_QK, COLS_QK, COLS_V, COLS_V, COLS_GATE)
IN_COLS = sum(IN_SPLITS)

kernel_name = 'hybrid_gated_conv_dilated_retention_moe'


def rmsnorm(x, g):
    xf = x.astype(jnp.float32)
    y = xf * lax.rsqrt(jnp.mean(xf * xf, axis=-1, keepdims=True) + EPS)
    return (y * g.astype(jnp.float32)).astype(x.dtype)


def modulate(h, shift, scale):
    return h * (1 + scale[:, None, :]) + shift[:, None, :]


def swiglu(h, w_gate, w_up, w_down):
    return (jax.nn.silu(h @ w_gate) * (h @ w_up)) @ w_down


def conv_branch(u, conv_w, conv_b, ln_g, ln_b):
    a, b = jnp.split(u, 2, axis=-1)
    z = a * jax.nn.sigmoid(b)
    z = lax.conv_general_dilated(
        z, conv_w[:, None, :].astype(z.dtype), window_strides=(1,),
        padding=((CONV_WIDTH - 1, 0),), dimension_numbers=('NWC', 'WIO', 'NWC'),
        feature_group_count=CONV_DIM) + conv_b
    zf = z.astype(jnp.float32)
    mu = jnp.mean(zf, axis=-1, keepdims=True)
    var = jnp.mean(jnp.square(zf - mu), axis=-1, keepdims=True)
    zn = (zf - mu) * lax.rsqrt(var + EPS) * ln_g + ln_b
    return jax.nn.silu(zn).astype(u.dtype)


def dilated_attention(q, k, v, window, dilation, slopes):
    B, S, H, dh = q.shape
    steps = window // dilation
    L = S // dilation
    nblk = -(-L // ATT_BLOCK)
    pad = nblk * ATT_BLOCK - L

    def stride_gather(a):
        a = a.reshape(B, L, dilation, H, dh).transpose(0, 2, 1, 3, 4)
        a = jnp.pad(a, ((0, 0), (0, 0), (0, pad), (0, 0), (0, 0)))
        return a.reshape(B, dilation, nblk, ATT_BLOCK, H, dh)

    def with_prev(a):
        prev = jnp.pad(a, ((0, 0), (0, 0), (1, 0), (0, 0), (0, 0), (0, 0)))[:, :, :-1]
        return jnp.concatenate([prev, a], axis=3)

    qb = stride_gather(q)
    kk = with_prev(stride_gather(k))
    vv = with_prev(stride_gather(v))
    s = jnp.einsum('brnqhd,brnkhd->brnhqk', qb, kk).astype(jnp.float32) * (dh ** -0.5)
    qi = jnp.arange(ATT_BLOCK)[:, None]
    ki = jnp.arange(2 * ATT_BLOCK)[None, :]
    step_dist = qi + ATT_BLOCK - ki
    kpos = jnp.arange(nblk)[:, None] * ATT_BLOCK + ki - ATT_BLOCK
    valid = ((step_dist >= 0) & (step_dist <= steps))[None] & (kpos >= 0)[:, None, :]
    alibi = -slopes[:, None, None] * (step_dist * dilation).astype(jnp.float32)
    s = jnp.where(valid[None, None, :, None], s + alibi[None, None, None], -jnp.inf)
    lse = jax.nn.logsumexp(s, axis=-1)
    p = jnp.exp(s - lse[..., None])
    o = jnp.einsum('brnhqk,brnkhd->brnqhd', p, vv.astype(jnp.float32))
    o = o.reshape(B, dilation, nblk * ATT_BLOCK, H, dh)[:, :, :L]
    o = o.transpose(0, 2, 1, 3, 4).reshape(B, S, H, dh)
    lse = lse.transpose(0, 1, 2, 4, 3).reshape(B, dilation, nblk * ATT_BLOCK, H)[:, :, :L]
    lse = lse.transpose(0, 2, 1, 3).reshape(B, S, H)
    return o, lse


def dilated_branch(u):
    B, S, _ = u.shape
    qkv = u.reshape(B, S, ATT_GROUPS, 3, ATT_HEADS_PER_GROUP, ATT_HEAD_DIM)
    slopes = 2.0 ** (-8.0 * jnp.arange(1, ATT_HEADS + 1, dtype=jnp.float32) / ATT_HEADS)
    outs, lses = [], []
    for gi, (window, dilation) in enumerate(ATT_PATTERNS):
        hs = slopes[gi * ATT_HEADS_PER_GROUP:(gi + 1) * ATT_HEADS_PER_GROUP]
        o, l = dilated_attention(qkv[:, :, gi, 0], qkv[:, :, gi, 1], qkv[:, :, gi, 2],
                                 window, dilation, hs)
        outs.append(o)
        lses.append(l)
    alpha = jax.nn.softmax(jnp.stack(lses), axis=0)
    o = jnp.einsum('gbsh,gbshd->bshd', alpha, jnp.stack(outs))
    return o.reshape(B, S, ATT_OUT_DIM).astype(u.dtype)


def retention(q, k, v):
    B, S, H, dk = q.shape
    dv = v.shape[-1]
    C = RET_CHUNK
    n = S // C
    log_gamma = jnp.log1p(-(2.0 ** (-5.0 - jnp.arange(H, dtype=jnp.float32))))
    idx = jnp.arange(C, dtype=jnp.float32)
    diff = idx[:, None] - idx[None, :]
    intra = jnp.where(diff >= 0, jnp.exp(log_gamma[:, None, None] * jnp.maximum(diff, 0.0)), 0.0)
    q_decay = jnp.exp(log_gamma[:, None] * (idx + 1.0))
    k_decay = jnp.exp(log_gamma[:, None] * (C - 1.0 - idx))
    chunk_decay = jnp.exp(log_gamma * C)

    def to_chunks(a):
        return a.astype(jnp.float32).reshape(B, n, C, H, -1).transpose(1, 0, 3, 2, 4)

    qc, kc, vc = to_chunks(q), to_chunks(k) * (dk ** -0.5), to_chunks(v)

    def step(state, inp):
        qi, ki, vi = inp
        s = jnp.einsum('bhqd,bhkd->bhqk', qi, ki) * intra[None]
        o = (jnp.einsum('bhqk,bhkv->bhqv', s, vi)
             + jnp.einsum('bhqd,bhdv->bhqv', qi, state) * q_decay[None, :, :, None])
        state = (state * chunk_decay[None, :, None, None]
                 + jnp.einsum('bhkd,bhkv->bhdv', ki * k_decay[None, :, :, None], vi))
        return state, o

    state0 = jnp.zeros((B, H, dk, dv), jnp.float32)
    _, o = lax.scan(step, state0, (qc, kc, vc))
    o = o.transpose(1, 0, 3, 2, 4).reshape(B, S, H, dv)
    return o * lax.rsqrt(jnp.mean(o * o, axis=-1, keepdims=True) + EPS)


def hybrid_mixer(h, w_in, conv_w, conv_b, ln_g, ln_b, w_proj_a, w_proj_b, w_proj_c, w_out):
    B, S, _ = h.shape
    u = h @ w_in
    offsets = [int(o) for o in np.cumsum(IN_SPLITS)[:-1]]
    u_a, u_b, u_q, u_k, u_v, u_g, u_gates = jnp.split(u, offsets, axis=-1)
    y_a = conv_branch(u_a, conv_w, conv_b, ln_g, ln_b) @ w_proj_a
    y_b = dilated_branch(u_b) @ w_proj_b
    ret = retention(u_q.reshape(B, S, RET_HEADS, RET_QK_DIM),
                    u_k.reshape(B, S, RET_HEADS, RET_QK_DIM),
                    u_v.reshape(B, S, RET_HEADS, RET_V_DIM))
    y_c = (jax.nn.silu(u_g) * ret.reshape(B, S, COLS_V).astype(h.dtype)) @ w_proj_c
    gates = jax.nn.sigmoid(u_gates.reshape(B, S, N_BRANCHES, D_MODEL))
    merged = gates[:, :, 0] * y_a + gates[:, :, 1] * y_b + gates[:, :, 2] * y_c
    return merged @ w_out


def moe_swiglu(h, w_router, b_router, w_gate, w_up, w_down):
    logits = (h @ w_router).astype(jnp.float32) + b_router
    top_vals, top_idx = lax.top_k(logits, TOP_K)
    top_w = jax.nn.softmax(top_vals, axis=-1)
    combine = jnp.sum(jax.nn.one_hot(top_idx, N_EXPERTS, dtype=jnp.float32) * top_w[..., None], axis=-2)
    out = jnp.zeros(h.shape, jnp.float32)
    for e in range(N_EXPERTS):
        out = out + combine[..., e:e + 1] * swiglu(h, w_gate[e], w_up[e], w_down[e]).astype(jnp.float32)
    return out.astype(h.dtype)


def setup_inputs(seed: int = 0) -> dict:
    key = jax.random.key(seed)
    ks = jax.random.split(key, 24)
    f32 = jnp.float32
    D = D_MODEL

    def nrm(k, shape, scale):
        return jax.random.normal(k, shape, f32) * scale

    return {
        'x': nrm(ks[0], (BATCH, SEQ, D), 1.0),
        'c': nrm(ks[1], (BATCH, D), 1.0),
        'w_ada': nrm(ks[2], (DEPTH, D, 6 * D), 0.5 * D ** -0.5),
        'b_ada': nrm(ks[3], (DEPTH, 6 * D), 0.01),
        'g_mix': 1.0 + nrm(ks[4], (DEPTH, D), 0.02),
        'w_in': nrm(ks[5], (DEPTH, D, IN_COLS), D ** -0.5),
        'conv_w': nrm(ks[6], (DEPTH, CONV_WIDTH, CONV_DIM), CONV_WIDTH ** -0.5),
        'conv_b': nrm(ks[7], (DEPTH, CONV_DIM), 0.01),
        'conv_ln_g': 1.0 + nrm(ks[8], (DEPTH, CONV_DIM), 0.02),
        'conv_ln_b': nrm(ks[9], (DEPTH, CONV_DIM), 0.01),
        'w_proj_a': nrm(ks[10], (DEPTH, CONV_DIM, D), CONV_DIM ** -0.5),
        'w_proj_b': nrm(ks[11], (DEPTH, ATT_OUT_DIM, D), ATT_OUT_DIM ** -0.5),
        'w_proj_c': nrm(ks[12], (DEPTH, COLS_V, D), COLS_V ** -0.5),
        'w_out': nrm(ks[13], (DEPTH, D, D), D ** -0.5),
        'g_ffn': 1.0 + nrm(ks[14], (DEPTH, D), 0.02),
        'w_ff_gate': nrm(ks[15], (N_DENSE, D, D_FF), D ** -0.5),
        'w_ff_up': nrm(ks[16], (N_DENSE, D, D_FF), D ** -0.5),
        'w_ff_down': nrm(ks[17], (N_DENSE, D_FF, D), D_FF ** -0.5),
        'w_router': nrm(ks[18], (N_MOE, D, N_EXPERTS), D ** -0.5),
        'b_router': nrm(ks[19], (N_MOE, N_EXPERTS), 0.01),
        'w_exp_gate': nrm(ks[20], (N_MOE, N_EXPERTS, D, D_EXPERT), D ** -0.5),
        'w_exp_up': nrm(ks[21], (N_MOE, N_EXPERTS, D, D_EXPERT), D ** -0.5),
        'w_exp_down': nrm(ks[22], (N_MOE, N_EXPERTS, D_EXPERT, D), D_EXPERT ** -0.5),
        'g_final': 1.0 + nrm(ks[23], (D,), 0.02),
    }


def reference(x, c, w_ada, b_ada, g_mix, w_in, conv_w, conv_b, conv_ln_g, conv_ln_b,
              w_proj_a, w_proj_b, w_proj_c, w_out, g_ffn, w_ff_gate, w_ff_up, w_ff_down,
              w_router, b_router, w_exp_gate, w_exp_up, w_exp_down, g_final):
    cond = jax.nn.silu(c)
    for l in range(DEPTH):
        mod = cond @ w_ada[l] + b_ada[l]
        sh1, sc1, gt1, sh2, sc2, gt2 = jnp.split(mod, 6, axis=-1)
        h = modulate(rmsnorm(x, g_mix[l]), sh1, sc1)
        y = hybrid_mixer(h, w_in[l], conv_w[l], conv_b[l], conv_ln_g[l], conv_ln_b[l],
                         w_proj_a[l], w_proj_b[l], w_proj_c[l], w_out[l])
        x = x + gt1[:, None, :] * y
        h = modulate(rmsnorm(x, g_ffn[l]), sh2, sc2)
        j = l // 2
        if l % 2 == 0:
            f = swiglu(h, w_ff_gate[j], w_ff_up[j], w_ff_down[j])
        else:
            f = moe_swiglu(h, w_router[j], b_router[j], w_exp_gate[j], w_exp_up[j], w_exp_down[j])
        x = x + gt2[:, None, :] * f
    return rmsnorm(x, g_final)
```

```python
import functools
import math

import numpy as np
import jax
import jax.numpy as jnp
from jax import lax
from jax.experimental import pallas as pl
from jax.experimental.pallas import tpu as pltpu

F32 = jnp.float32
BF16 = jnp.bfloat16

D_MODEL = 2048
CONV_DIM = 1536
CONV_WIDTH = 31
ATT_HEAD_DIM = 128
ATT_HEADS_PER_GROUP = 4
ATT_PATTERNS = ((128, 1), (512, 4), (2048, 16))
ATT_GROUPS = len(ATT_PATTERNS)
ATT_HEADS = ATT_GROUPS * ATT_HEADS_PER_GROUP
ATT_OUT_DIM = ATT_HEADS_PER_GROUP * ATT_HEAD_DIM
ATT_BLOCK = 128
RET_HEADS = 8
RET_QK_DIM = 128
RET_V_DIM = 256
RET_CHUNK = 128
N_BRANCHES = 3
N_EXPERTS = 8
TOP_K = 2
EPS = 1e-6
COLS_A = 2 * CONV_DIM
COLS_B = 3 * ATT_HEADS * ATT_HEAD_DIM
COLS_QK = RET_HEADS * RET_QK_DIM
COLS_V = RET_HEADS * RET_V_DIM
COLS_GATE = N_BRANCHES * D_MODEL
OFF_B = COLS_A
OFF_Q = OFF_B + COLS_B
OFF_K = OFF_Q + COLS_QK
OFF_V = OFF_K + COLS_QK
OFF_G = OFF_V + COLS_V
OFF_GATES = OFF_G + COLS_V
IN_COLS = OFF_GATES + COLS_GATE

V7X_VMEM_BYTES = 64 * 1024 * 1024
V7X_LANES = 128
V7X_SUBLANES = 8
VMEM_LIMIT = V7X_VMEM_BYTES * 7 // 8

COL_BLK = 512
TM_NORM = 512
TM_MM = 1024
TN_IN = 1536
TM_FF = 512
TF_FF = 512
TM_EXP = 512
TS_CONV = 256
CONV_HALO = 32
CONV_ROWS = 32
CONV_LANES = 256
TS_MOVE = 512
NEG = -1e30


def _params(sem):
    return pltpu.CompilerParams(dimension_semantics=sem, vmem_limit_bytes=VMEM_LIMIT)


def _silu(v):
    return v * jax.nn.sigmoid(v)


def _ada_kernel(c_ref, w_ref, b_ref, o_ref):
    cond = _silu(c_ref[...])
    o_ref[0] = jnp.dot(cond.astype(BF16), w_ref[0].astype(BF16), preferred_element_type=F32) + b_ref[0]


def _ada(c, w_ada, b_ada):
    depth, d, n = w_ada.shape
    b = c.shape[0]
    assert b <= V7X_SUBLANES
    c8 = jnp.zeros((V7X_SUBLANES, d), F32).at[:b].set(c)
    tn = 1536
    out = pl.pallas_call(
        _ada_kernel,
        grid=(depth, n // tn),
        in_specs=[pl.BlockSpec((V7X_SUBLANES, d), lambda l, j: (0, 0)),
                  pl.BlockSpec((1, d, tn), lambda l, j: (l, 0, j)),
                  pl.BlockSpec((1, 1, tn), lambda l, j: (l, 0, j))],
        out_specs=pl.BlockSpec((1, V7X_SUBLANES, tn), lambda l, j: (l, 0, j)),
        out_shape=jax.ShapeDtypeStruct((depth, V7X_SUBLANES, n), F32),
        compiler_params=_params(("arbitrary", "arbitrary")),
        name="ada",
    )(c8, w_ada, b_ada.reshape(depth, 1, n))
    return out[:, :b, :].reshape(depth * b * 6, 1, d)


def _mod_spec(base, which, tiles_per_batch, width=D_MODEL):
    return pl.BlockSpec((1, 1, width), lambda i, *_: (base + (i // tiles_per_batch) * 6 + which, 0, 0))


def _norm_mod_value(x, g, sc, sh):
    ms = jnp.mean(x * x, axis=-1, keepdims=True)
    return (x * lax.rsqrt(ms + EPS) * g) * (1.0 + sc) + sh


def _norm_mod_kernel(x_ref, g_ref, sc_ref, sh_ref, o_ref):
    o_ref[...] = _norm_mod_value(x_ref[...], g_ref[...], sc_ref[0], sh_ref[0]).astype(o_ref.dtype)


def _norm_mod(x, g, mod, base, which_shift, which_scale, seq):
    t, d = x.shape
    tpb = seq // TM_NORM
    return pl.pallas_call(
        _norm_mod_kernel,
        grid=(t // TM_NORM,),
        in_specs=[pl.BlockSpec((TM_NORM, d), lambda i: (i, 0)),
                  pl.BlockSpec((1, d), lambda i: (0, 0)),
                  _mod_spec(base, which_scale, tpb),
                  _mod_spec(base, which_shift, tpb)],
        out_specs=pl.BlockSpec((TM_NORM, d), lambda i: (i, 0)),
        out_shape=jax.ShapeDtypeStruct((t, d), BF16),
        compiler_params=_params(("arbitrary",)),
        name="norm_mod",
    )(x, g.reshape(1, d), mod, mod)


def _matmul_kernel(a_ref, b_ref, o_ref):
    o_ref[...] = jnp.dot(a_ref[...], b_ref[...], preferred_element_type=F32).astype(o_ref.dtype)


def _matmul(a, w, tn, out_dtype):
    m, k = a.shape
    n = w.shape[1]
    return pl.pallas_call(
        _matmul_kernel,
        grid=(m // TM_MM, n // tn),
        in_specs=[pl.BlockSpec((TM_MM, k), lambda i, j: (i, 0)),
                  pl.BlockSpec((k, tn), lambda i, j: (0, j))],
        out_specs=pl.BlockSpec((TM_MM, tn), lambda i, j: (i, j)),
        out_shape=jax.ShapeDtypeStruct((m, n), out_dtype),
        compiler_params=_params(("arbitrary", "arbitrary")),
        name="in_proj",
    )(a, w)


def _conv_kernel(a_ref, b_ref, w_ref, cb_ref, lg_ref, lb_ref, o_ref, zs_ref, y_ref):
    ts = a_ref.shape[0]
    c = a_ref.shape[1]

    @pl.when(pl.program_id(1) == 0)
    def _():
        zs_ref[0, 0:CONV_HALO, :] = jnp.zeros((CONV_HALO, c), F32)

    zs_ref[0, CONV_HALO:CONV_HALO + ts, :] = a_ref[...].astype(F32) * jax.nn.sigmoid(b_ref[...].astype(F32))
    span = ts + CONV_HALO - V7X_SUBLANES
    for r in range(1, V7X_SUBLANES):
        zs_ref[r, 0:span, :] = zs_ref[0, r:r + span, :]

    first = CONV_HALO - (CONV_WIDTH - 1)

    def rows(i, carry):
        r0 = pl.multiple_of(i * CONV_ROWS, CONV_ROWS)
        for lc in range(c // CONV_LANES):
            ls = slice(lc * CONV_LANES, (lc + 1) * CONV_LANES)
            acc = jnp.broadcast_to(cb_ref[:, ls], (CONV_ROWS, CONV_LANES))
            for k in range(CONV_WIDTH):
                off = first + k
                acc = acc + w_ref[k:k + 1, ls] * zs_ref[off % V7X_SUBLANES,
                                                         pl.ds(r0 + (off // V7X_SUBLANES) * V7X_SUBLANES, CONV_ROWS), ls]
            y_ref[pl.ds(r0, CONV_ROWS), ls] = acc
        return carry

    lax.fori_loop(0, ts // CONV_ROWS, rows, 0)

    zs_ref[0, 0:CONV_HALO, :] = zs_ref[0, ts:ts + CONV_HALO, :]

    y = y_ref[...]
    mu = jnp.mean(y, axis=-1, keepdims=True)
    yc = y - mu
    var = jnp.mean(yc * yc, axis=-1, keepdims=True)
    zn = yc * lax.rsqrt(var + EPS) * lg_ref[...] + lb_ref[...]
    o_ref[...] = _silu(zn).astype(o_ref.dtype)


def _conv_branch(u, conv_w, conv_b, ln_g, ln_b, batch, seq):
    t = u.shape[0]
    c = CONV_DIM
    spb = seq // TS_CONV
    w = jnp.zeros((CONV_HALO, c), F32).at[:CONV_WIDTH].set(conv_w)
    return pl.pallas_call(
        _conv_kernel,
        grid=(batch, spb),
        in_specs=[pl.BlockSpec((TS_CONV, c), lambda b, s: (b * spb + s, 0)),
                  pl.BlockSpec((TS_CONV, c), lambda b, s: (b * spb + s, 1)),
                  pl.BlockSpec((CONV_HALO, c), lambda b, s: (0, 0)),
                  pl.BlockSpec((1, c), lambda b, s: (0, 0)),
                  pl.BlockSpec((1, c), lambda b, s: (0, 0)),
                  pl.BlockSpec((1, c), lambda b, s: (0, 0))],
        out_specs=pl.BlockSpec((TS_CONV, c), lambda b, s: (b * spb + s, 0)),
        out_shape=jax.ShapeDtypeStruct((t, c), BF16),
        scratch_shapes=[pltpu.VMEM((V7X_SUBLANES, TS_CONV + CONV_HALO, c), F32),
                        pltpu.VMEM((TS_CONV, c), F32)],
        compiler_params=_params(("arbitrary", "arbitrary")),
        name="conv_branch",
    )(u, u, w, conv_b.reshape(1, c), ln_g.reshape(1, c), ln_b.reshape(1, c))


def _attn_kernel(q_ref, kc_ref, kp_ref, vc_ref, vp_ref, o_ref, l_ref, *, slopes, dilation):
    n = pl.program_id(2)
    blk = ATT_BLOCK
    qi = lax.broadcasted_iota(jnp.int32, (blk, 2 * blk), 0)
    ki = lax.broadcasted_iota(jnp.int32, (blk, 2 * blk), 1)
    step = qi + blk - ki
    limit = jnp.where(ki >= blk, blk, jnp.where(n > 0, blk, 0))
    valid = jnp.logical_and(step >= 0, step <= limit)
    dist = (step * dilation).astype(F32)
    scale = ATT_HEAD_DIM ** -0.5
    for hh in range(ATT_HEADS_PER_GROUP):
        sl = slice(hh * ATT_HEAD_DIM, (hh + 1) * ATT_HEAD_DIM)
        q = q_ref[:, sl]
        k = jnp.concatenate([kp_ref[:, sl], kc_ref[:, sl]], axis=0)
        v = jnp.concatenate([vp_ref[:, sl], vc_ref[:, sl]], axis=0)
        s = lax.dot_general(q, k, (((1,), (1,)), ((), ())), preferred_element_type=F32) * scale
        s = jnp.where(valid, s - slopes[hh] * dist, NEG)
        m = jnp.max(s, axis=-1, keepdims=True)
        p = jnp.exp(s - m)
        denom = jnp.sum(p, axis=-1, keepdims=True)
        o = jnp.dot(p.astype(BF16), v, preferred_element_type=F32) / denom
        o_ref[:, sl] = o
        l_ref[:, sl] = jnp.broadcast_to(m + jnp.log(denom), (blk, ATT_HEAD_DIM))


def _attn_group(u, gi, batch, seq):
    window, dilation = ATT_PATTERNS[gi]
    assert window // dilation == ATT_BLOCK and seq % (dilation * ATT_BLOCK) == 0
    length = seq // dilation
    nblk = length // ATT_BLOCK
    width = ATT_OUT_DIM
    assert width == COL_BLK
    row_blocks = IN_COLS // COL_BLK
    u3 = u.reshape(batch, length, dilation * IN_COLS)
    base = OFF_B // COL_BLK + gi * 3
    slopes = tuple(2.0 ** (-8.0 * (gi * ATT_HEADS_PER_GROUP + hh + 1) / ATT_HEADS)
                   for hh in range(ATT_HEADS_PER_GROUP))

    def spec(which, prev):
        def imap(b, r, n):
            return (b, jnp.maximum(n - 1, 0) if prev else n, r * row_blocks + base + which)
        return pl.BlockSpec((None, ATT_BLOCK, width), imap)

    out_spec = pl.BlockSpec((None, ATT_BLOCK, width), lambda b, r, n: (b, n, r))
    shape = jax.ShapeDtypeStruct((batch, length, dilation * width), F32)
    o, lse = pl.pallas_call(
        functools.partial(_attn_kernel, slopes=slopes, dilation=dilation),
        grid=(batch, dilation, nblk),
        in_specs=[spec(0, False), spec(1, False), spec(1, True), spec(2, False), spec(2, True)],
        out_specs=[out_spec, out_spec],
        out_shape=[shape, shape],
        compiler_params=_params(("arbitrary", "arbitrary", "arbitrary")),
        name=f"attn_g{gi}",
    )(u3, u3, u3, u3, u3)
    return o.reshape(batch * seq, width), lse.reshape(batch * seq, width)


def _attn_combine_kernel(*refs):
    o_refs, l_refs, out_ref = refs[0:ATT_GROUPS], refs[ATT_GROUPS:2 * ATT_GROUPS], refs[-1]
    ls = [r[...] for r in l_refs]
    m = functools.reduce(jnp.maximum, ls)
    es = [jnp.exp(l - m) for l in ls]
    num = functools.reduce(jnp.add, [e * r[...] for e, r in zip(es, o_refs)])
    out_ref[...] = (num / functools.reduce(jnp.add, es)).astype(out_ref.dtype)


def _attn_combine(outs, lses):
    t, w = outs[0].shape
    spec = pl.BlockSpec((TM_MM, w), lambda i: (i, 0))
    return pl.pallas_call(
        _attn_combine_kernel,
        grid=(t // TM_MM,),
        in_specs=[spec] * (2 * ATT_GROUPS),
        out_specs=spec,
        out_shape=jax.ShapeDtypeStruct((t, w), BF16),
        compiler_params=_params(("arbitrary",)),
        name="attn_combine",
    )(*outs, *lses)


RET_HEADS_PER_STEP = 4


def _retention_tables():
    h = np.arange(RET_HEADS, dtype=np.float64)
    log_gamma = np.log1p(-(2.0 ** (-5.0 - h)))
    idx = np.arange(RET_CHUNK, dtype=np.float64)
    diff = idx[:, None] - idx[None, :]
    scale = RET_QK_DIM ** -0.5
    intra = np.where(diff >= 0, np.exp(log_gamma[:, None, None] * np.maximum(diff, 0.0)), 0.0) * scale
    q_decay = np.exp(log_gamma[:, None] * (idx + 1.0))
    k_decay = np.exp(log_gamma[:, None] * (RET_CHUNK - 1.0 - idx)) * scale
    chunk_decay = np.exp(log_gamma * RET_CHUNK)
    qd = np.broadcast_to(q_decay[:, :, None], (RET_HEADS, RET_CHUNK, RET_QK_DIM))
    kd = np.broadcast_to(k_decay[:, :, None], (RET_HEADS, RET_CHUNK, RET_QK_DIM))
    cd = np.broadcast_to(chunk_decay[:, None, None], (RET_HEADS, 1, RET_V_DIM))
    return tuple(jnp.asarray(a, F32) for a in (intra, qd, kd, cd))


def _retention_kernel(q_ref, k_ref, va_ref, vb_ref, ga_ref, gb_ref, intra_ref, qd_ref, kd_ref, cd_ref,
                      o_ref, st_ref):
    @pl.when(pl.program_id(2) == 0)
    def _():
        st_ref[...] = jnp.zeros(st_ref.shape, F32)

    half = RET_HEADS_PER_STEP // 2
    for hh in range(RET_HEADS_PER_STEP):
        qs = slice(hh * RET_QK_DIM, (hh + 1) * RET_QK_DIM)
        vs = slice((hh % half) * RET_V_DIM, (hh % half + 1) * RET_V_DIM)
        q = q_ref[:, qs]
        k = k_ref[:, qs]
        v = (va_ref if hh < half else vb_ref)[:, vs]
        g = (ga_ref if hh < half else gb_ref)[:, vs].astype(F32)
        s = lax.dot_general(q, k, (((1,), (1,)), ((), ())), preferred_element_type=F32) * intra_ref[hh]
        q_dec = (q.astype(F32) * qd_ref[hh]).astype(BF16)
        st = st_ref[hh]
        o = (jnp.dot(s.astype(BF16), v, preferred_element_type=F32)
             + jnp.dot(q_dec, st.astype(BF16), preferred_element_type=F32))
        k_dec = (k.astype(F32) * kd_ref[hh]).T.astype(BF16)
        st_ref[hh] = st * cd_ref[hh] + jnp.dot(k_dec, v, preferred_element_type=F32)
        on = o * lax.rsqrt(jnp.mean(o * o, axis=-1, keepdims=True) + EPS)
        o_ref[:, hh * RET_V_DIM:(hh + 1) * RET_V_DIM] = (_silu(g) * on).astype(o_ref.dtype)


def _retention_branch(u, batch, seq):
    t = u.shape[0]
    hps = RET_HEADS_PER_STEP
    assert hps * RET_QK_DIM == COL_BLK and (hps // 2) * RET_V_DIM == COL_BLK
    nchunk = seq // RET_CHUNK
    groups = RET_HEADS // hps

    def col(off, mult, add):
        return pl.BlockSpec((RET_CHUNK, COL_BLK), lambda b, hg, c: (b * nchunk + c, off // COL_BLK + mult * hg + add))

    def table(shape):
        return pl.BlockSpec((hps,) + shape, lambda b, hg, c: (hg, 0, 0))

    return pl.pallas_call(
        _retention_kernel,
        grid=(batch, groups, nchunk),
        in_specs=[col(OFF_Q, 1, 0), col(OFF_K, 1, 0), col(OFF_V, 2, 0), col(OFF_V, 2, 1),
                  col(OFF_G, 2, 0), col(OFF_G, 2, 1),
                  table((RET_CHUNK, RET_CHUNK)), table((RET_CHUNK, RET_QK_DIM)), table((RET_CHUNK, RET_QK_DIM)),
                  table((1, RET_V_DIM))],
        out_specs=pl.BlockSpec((RET_CHUNK, hps * RET_V_DIM), lambda b, hg, c: (b * nchunk + c, hg)),
        out_shape=jax.ShapeDtypeStruct((t, COLS_V), BF16),
        scratch_shapes=[pltpu.VMEM((hps, RET_QK_DIM, RET_V_DIM), F32)],
        compiler_params=_params(("arbitrary", "arbitrary", "arbitrary")),
        name="retention",
    )(u, u, u, u, u, u, *_retention_tables())


def _merge_kernel(ca_ref, ob_ref, rc_ref, wa_ref, wb_ref, wc_ref, g0_ref, g1_ref, g2_ref, o_ref):
    acc = None
    for x_ref, w_ref, g_ref in ((ca_ref, wa_ref, g0_ref), (ob_ref, wb_ref, g1_ref), (rc_ref, wc_ref, g2_ref)):
        y = jnp.dot(x_ref[...], w_ref[...], preferred_element_type=F32)
        term = jax.nn.sigmoid(g_ref[...].astype(F32)) * y
        acc = term if acc is None else acc + term
    o_ref[...] = acc.astype(o_ref.dtype)


def _merge(ca, ob, rc, wa, wb, wc, u):
    t = ca.shape[0]
    d = wa.shape[1]
    tn = COL_BLK
    gate_blk = OFF_GATES // tn
    per_gate = d // tn

    def lhs(a):
        return pl.BlockSpec((TM_MM, a.shape[1]), lambda i, j: (i, 0))

    def rhs(w):
        return pl.BlockSpec((w.shape[0], tn), lambda i, j: (0, j))

    def gate(k):
        return pl.BlockSpec((TM_MM, tn), lambda i, j: (i, gate_blk + k * per_gate + j))

    return pl.pallas_call(
        _merge_kernel,
        grid=(t // TM_MM, d // tn),
        in_specs=[lhs(ca), lhs(ob), lhs(rc), rhs(wa), rhs(wb), rhs(wc), gate(0), gate(1), gate(2)],
        out_specs=pl.BlockSpec((TM_MM, tn), lambda i, j: (i, j)),
        out_shape=jax.ShapeDtypeStruct((t, d), BF16),
        compiler_params=_params(("arbitrary", "arbitrary")),
        name="merge",
    )(ca, ob, rc, wa, wb, wc, u, u, u)


def _matmul_residual_kernel(a_ref, w_ref, x_ref, gt_ref, o_ref):
    y = jnp.dot(a_ref[...], w_ref[...], preferred_element_type=F32)
    o_ref[...] = x_ref[...] + gt_ref[0] * y


def _matmul_residual(a, w, x, mod, base, which, seq):
    t, k = a.shape
    d = w.shape[1]
    tn = COL_BLK
    tpb = seq // TM_MM
    return pl.pallas_call(
        _matmul_residual_kernel,
        grid=(t // TM_MM, d // tn),
        in_specs=[pl.BlockSpec((TM_MM, k), lambda i, j: (i, 0)),
                  pl.BlockSpec((k, tn), lambda i, j: (0, j)),
                  pl.BlockSpec((TM_MM, tn), lambda i, j: (i, j)),
                  pl.BlockSpec((1, 1, tn), lambda i, j: (base + (i // tpb) * 6 + which, 0, j))],
        out_specs=pl.BlockSpec((TM_MM, tn), lambda i, j: (i, j)),
        out_shape=jax.ShapeDtypeStruct((t, d), F32),
        compiler_params=_params(("arbitrary", "arbitrary")),
        name="out_proj",
    )(a, w, x, mod)


def _swiglu_step(h, wg_ref, wu_ref, wd_ref):
    a = jnp.dot(h, wg_ref[...], preferred_element_type=F32)
    b = jnp.dot(h, wu_ref[...], preferred_element_type=F32)
    return jnp.dot((_silu(a) * b).astype(BF16), wd_ref[...], preferred_element_type=F32)


def _dense_ffn_kernel(x_ref, g_ref, sc_ref, sh_ref, gt_ref, wg_ref, wu_ref, wd_ref, o_ref, h_ref):
    j = pl.program_id(1)

    @pl.when(j == 0)
    def _():
        h_ref[...] = _norm_mod_value(x_ref[...], g_ref[...], sc_ref[0], sh_ref[0]).astype(BF16)

    part = _swiglu_step(h_ref[...], wg_ref, wu_ref, wd_ref)

    @pl.when(j == 0)
    def _():
        o_ref[...] = part

    @pl.when(j > 0)
    def _():
        o_ref[...] += part

    @pl.when(j == pl.num_programs(1) - 1)
    def _():
        o_ref[...] = x_ref[...] + gt_ref[0] * o_ref[...]


def _dense_ffn(x, g, mod, base, wg, wu, wd, seq):
    t, d = x.shape
    f = wg.shape[1]
    tpb = seq // TM_FF
    row = pl.BlockSpec((TM_FF, d), lambda i, j: (i, 0))
    return pl.pallas_call(
        _dense_ffn_kernel,
        grid=(t // TM_FF, f // TF_FF),
        in_specs=[row,
                  pl.BlockSpec((1, d), lambda i, j: (0, 0)),
                  _mod_spec(base, 4, tpb), _mod_spec(base, 3, tpb), _mod_spec(base, 5, tpb),
                  pl.BlockSpec((d, TF_FF), lambda i, j: (0, j)),
                  pl.BlockSpec((d, TF_FF), lambda i, j: (0, j)),
                  pl.BlockSpec((TF_FF, d), lambda i, j: (j, 0))],
        out_specs=row,
        out_shape=jax.ShapeDtypeStruct((t, d), F32),
        scratch_shapes=[pltpu.VMEM((TM_FF, d), BF16)],
        compiler_params=_params(("arbitrary", "arbitrary")),
        name="dense_ffn",
    )(x, g.reshape(1, d), mod, mod, mod, wg, wu, wd)


META_E1, META_E2, META_R1, META_R2, META_W1, META_W2 = range(6)


def _router_kernel(x_ref, g_ref, sc_ref, sh_ref, wr_ref, br_ref, h_ref, meta_ref, cnt_ref):
    tm = x_ref.shape[0]

    @pl.when(pl.program_id(0) == 0)
    def _():
        cnt_ref[...] = jnp.zeros(cnt_ref.shape, F32)

    h = _norm_mod_value(x_ref[...], g_ref[...], sc_ref[0], sh_ref[0])
    h_ref[...] = h
    lane = lax.broadcasted_iota(jnp.int32, (tm, V7X_LANES), 1).astype(F32)
    logits = jnp.dot(h.astype(BF16), wr_ref[...], preferred_element_type=F32) + br_ref[...]
    logits = jnp.where(lane < N_EXPERTS, logits, NEG)
    m1 = jnp.max(logits, axis=-1, keepdims=True)
    i1 = jnp.min(jnp.where(logits == m1, lane, float(V7X_LANES)), axis=-1, keepdims=True)
    rest = jnp.where(lane == i1, NEG, logits)
    m2 = jnp.max(rest, axis=-1, keepdims=True)
    i2 = jnp.min(jnp.where(rest == m2, lane, float(V7X_LANES)), axis=-1, keepdims=True)
    e21 = jnp.exp(m2 - m1)
    w1 = 1.0 / (1.0 + e21)
    w2 = e21 * w1
    hot1 = lane == i1
    hot2 = lane == i2
    hot = jnp.where(jnp.logical_or(hot1, hot2), 1.0, 0.0)
    ri = lax.broadcasted_iota(jnp.int32, (tm, tm), 0)
    ci = lax.broadcasted_iota(jnp.int32, (tm, tm), 1)
    earlier = jnp.where(ci < ri, 1.0, 0.0).astype(BF16)
    before = jnp.dot(earlier, hot.astype(BF16), preferred_element_type=F32) + cnt_ref[...]
    r1 = jnp.sum(jnp.where(hot1, before, 0.0), axis=-1, keepdims=True)
    r2 = jnp.sum(jnp.where(hot2, before, 0.0), axis=-1, keepdims=True)
    cnt_ref[...] += jnp.sum(hot, axis=0, keepdims=True)
    meta = jnp.zeros((tm, V7X_LANES), F32)
    for slot, val in ((META_E1, i1), (META_E2, i2), (META_R1, r1), (META_R2, r2),
                      (META_W1, w1), (META_W2, w2)):
        meta = jnp.where(lane == slot, val, meta)
    meta_ref[...] = meta


def _router(x, g, mod, base, w_router, b_router, seq):
    t, d = x.shape
    tpb = seq // TM_NORM
    wr = jnp.zeros((d, V7X_LANES), BF16).at[:, :N_EXPERTS].set(w_router.astype(BF16))
    br = jnp.zeros((1, V7X_LANES), F32).at[0, :N_EXPERTS].set(b_router)
    return pl.pallas_call(
        _router_kernel,
        grid=(t // TM_NORM,),
        in_specs=[pl.BlockSpec((TM_NORM, d), lambda i: (i, 0)),
                  pl.BlockSpec((1, d), lambda i: (0, 0)),
                  _mod_spec(base, 4, tpb), _mod_spec(base, 3, tpb),
                  pl.BlockSpec((d, V7X_LANES), lambda i: (0, 0)),
                  pl.BlockSpec((1, V7X_LANES), lambda i: (0, 0))],
        out_specs=[pl.BlockSpec((TM_NORM, d), lambda i: (i, 0)),
                   pl.BlockSpec((TM_NORM, V7X_LANES), lambda i: (i, 0)),
                   pl.BlockSpec((1, V7X_LANES), lambda i: (0, 0))],
        out_shape=[jax.ShapeDtypeStruct((t, d), F32),
                   jax.ShapeDtypeStruct((t, V7X_LANES), F32),
                   jax.ShapeDtypeStruct((1, V7X_LANES), F32)],
        compiler_params=_params(("arbitrary",)),
        name="router",
    )(x, g.reshape(1, d), mod, mod, wr, br)


def _scatter_kernel(pos_ref, h_ref, xs_in_ref, xs_ref, sem):
    del xs_in_ref
    ts = h_ref.shape[0]
    t_total = pl.num_programs(0) * ts
    base = pl.program_id(0) * ts

    def copy(r, k):
        dst = pos_ref[k * t_total + base + r]
        return pltpu.make_async_copy(h_ref.at[pl.ds(r, 1)], xs_ref.at[pl.ds(dst, 1)], sem)

    def start(r, carry):
        copy(r, 0).start()
        copy(r, 1).start()
        return carry

    def wait(r, carry):
        copy(r, 0).wait()
        copy(r, 1).wait()
        return carry

    lax.fori_loop(0, ts, start, 0)
    lax.fori_loop(0, ts, wait, 0)


def _scatter_rows(h, pos, rows):
    t, d = h.shape
    return pl.pallas_call(
        _scatter_kernel,
        grid_spec=pltpu.PrefetchScalarGridSpec(
            num_scalar_prefetch=1,
            grid=(t // TS_MOVE,),
            in_specs=[pl.BlockSpec((TS_MOVE, d), lambda i, pos: (i, 0)),
                      pl.BlockSpec(memory_space=pl.ANY)],
            out_specs=pl.BlockSpec(memory_space=pl.ANY),
            scratch_shapes=[pltpu.SemaphoreType.DMA(())]),
        out_shape=jax.ShapeDtypeStruct((rows, d), F32),
        input_output_aliases={2: 0},
        compiler_params=_params(("arbitrary",)),
        name="expert_scatter",
    )(pos, h, jnp.zeros((rows, d), F32))


def _expert_ffn_kernel(te_ref, na_ref, xs_ref, wg_ref, wu_ref, wd_ref, o_ref, h_ref):
    i = pl.program_id(0)
    j = pl.program_id(1)
    active = i < na_ref[0]

    @pl.when(jnp.logical_and(active, j == 0))
    def _():
        h_ref[...] = xs_ref[...].astype(BF16)

    @pl.when(jnp.logical_and(jnp.logical_not(active), j == 0))
    def _():
        o_ref[...] = jnp.zeros(o_ref.shape, F32)

    @pl.when(active)
    def _():
        part = _swiglu_step(h_ref[...], wg_ref.at[0], wu_ref.at[0], wd_ref.at[0])

        @pl.when(j == 0)
        def _():
            o_ref[...] = part

        @pl.when(j > 0)
        def _():
            o_ref[...] += part


def _expert_ffn(xs, tile_expert, n_active, wg, wu, wd):
    rows, d = xs.shape
    f = wg.shape[2]
    nf = f // TF_FF

    def jcol(i, j, na):
        return jnp.where(i < na[0], j, nf - 1)

    row = pl.BlockSpec((TM_EXP, d), lambda i, j, te, na: (i, 0))
    return pl.pallas_call(
        _expert_ffn_kernel,
        grid_spec=pltpu.PrefetchScalarGridSpec(
            num_scalar_prefetch=2,
            grid=(rows // TM_EXP, nf),
            in_specs=[row,
                      pl.BlockSpec((1, d, TF_FF), lambda i, j, te, na: (te[i], 0, jcol(i, j, na))),
                      pl.BlockSpec((1, d, TF_FF), lambda i, j, te, na: (te[i], 0, jcol(i, j, na))),
                      pl.BlockSpec((1, TF_FF, d), lambda i, j, te, na: (te[i], jcol(i, j, na), 0))],
            out_specs=row,
            scratch_shapes=[pltpu.VMEM((TM_EXP, d), BF16)]),
        out_shape=jax.ShapeDtypeStruct((rows, d), F32),
        compiler_params=_params(("arbitrary", "arbitrary")),
        name="expert_ffn",
    )(tile_expert, n_active, xs, wg, wu, wd)


def _combine_kernel(pos_ref, x_ref, meta_ref, gt_ref, gf_ref, ys_ref, o_ref, y1_ref, y2_ref, sem, *, final_norm):
    ts = x_ref.shape[0]
    t_total = pl.num_programs(0) * ts
    base = pl.program_id(0) * ts

    def copy(r, k):
        src = pos_ref[k * t_total + base + r]
        dst = y1_ref if k == 0 else y2_ref
        return pltpu.make_async_copy(ys_ref.at[pl.ds(src, 1)], dst.at[pl.ds(r, 1)], sem)

    def start(r, carry):
        copy(r, 0).start()
        copy(r, 1).start()
        return carry

    def wait(r, carry):
        copy(r, 0).wait()
        copy(r, 1).wait()
        return carry

    lax.fori_loop(0, ts, start, 0)
    lax.fori_loop(0, ts, wait, 0)
    meta = meta_ref[...]
    w1 = meta[:, META_W1:META_W1 + 1]
    w2 = meta[:, META_W2:META_W2 + 1]
    f = w1 * y1_ref[...] + w2 * y2_ref[...]
    xn = x_ref[...] + gt_ref[0] * f
    if final_norm:
        xn = xn * lax.rsqrt(jnp.mean(xn * xn, axis=-1, keepdims=True) + EPS) * gf_ref[...]
    o_ref[...] = xn


def _combine(x, meta, pos, ys, mod, base, g_final, seq, final_norm):
    t, d = x.shape
    tpb = seq // TS_MOVE
    row = pl.BlockSpec((TS_MOVE, d), lambda i, pos: (i, 0))
    return pl.pallas_call(
        functools.partial(_combine_kernel, final_norm=final_norm),
        grid_spec=pltpu.PrefetchScalarGridSpec(
            num_scalar_prefetch=1,
            grid=(t // TS_MOVE,),
            in_specs=[row,
                      pl.BlockSpec((TS_MOVE, V7X_LANES), lambda i, pos: (i, 0)),
                      _mod_spec(base, 5, tpb),
                      pl.BlockSpec((1, d), lambda i, pos: (0, 0)),
                      pl.BlockSpec(memory_space=pl.ANY)],
            out_specs=row,
            scratch_shapes=[pltpu.VMEM((TS_MOVE, d), F32), pltpu.VMEM((TS_MOVE, d), F32),
                            pltpu.SemaphoreType.DMA(())]),
        out_shape=jax.ShapeDtypeStruct((t, d), F32),
        compiler_params=_params(("arbitrary",)),
        name="expert_combine",
    )(pos, x, meta, mod, g_final.reshape(1, d), ys)


def _moe_layer(x, g, mod, base, w_router, b_router, wg, wu, wd, g_final, seq, final_norm):
    t, d = x.shape
    h, meta, counts = _router(x, g, mod, base, w_router, b_router, seq)
    counts = counts[0, :N_EXPERTS].astype(jnp.int32)
    tiles = (counts + TM_EXP - 1) // TM_EXP
    tile_end = jnp.cumsum(tiles)
    offsets = (tile_end - tiles) * TM_EXP
    n_tiles = (TOP_K * t) // TM_EXP + N_EXPERTS
    n_active = tile_end[-1:]
    tile_ids = jnp.minimum(jnp.arange(n_tiles, dtype=jnp.int32), n_active[0] - 1)
    tile_expert = jnp.sum(tile_ids[:, None] >= tile_end[None, :], axis=1).astype(jnp.int32)
    e1 = meta[:, META_E1].astype(jnp.int32)
    e2 = meta[:, META_E2].astype(jnp.int32)
    pos = jnp.concatenate([offsets[e1] + meta[:, META_R1].astype(jnp.int32),
                           offsets[e2] + meta[:, META_R2].astype(jnp.int32)])
    xs = _scatter_rows(h, pos, n_tiles * TM_EXP)
    ys = _expert_ffn(xs, tile_expert, n_active.astype(jnp.int32), wg, wu, wd)
    return _combine(x, meta, pos, ys, mod, base, g_final, seq, final_norm)


def _final_norm_kernel(x_ref, g_ref, o_ref):
    x = x_ref[...]
    o_ref[...] = x * lax.rsqrt(jnp.mean(x * x, axis=-1, keepdims=True) + EPS) * g_ref[...]


def _final_norm(x, g):
    t, d = x.shape
    row = pl.BlockSpec((TM_NORM, d), lambda i: (i, 0))
    return pl.pallas_call(
        _final_norm_kernel,
        grid=(t // TM_NORM,),
        in_specs=[row, pl.BlockSpec((1, d), lambda i: (0, 0))],
        out_specs=row,
        out_shape=jax.ShapeDtypeStruct((t, d), F32),
        compiler_params=_params(("arbitrary",)),
        name="final_norm",
    )(x, g.reshape(1, d))


def kernel(x, c, w_ada, b_ada, g_mix, w_in, conv_w, conv_b, conv_ln_g, conv_ln_b, w_proj_a, w_proj_b, w_proj_c,
           w_out, g_ffn, w_ff_gate, w_ff_up, w_ff_down, w_router, b_router, w_exp_gate, w_exp_up, w_exp_down,
           g_final):
    batch, seq, d = x.shape
    depth = w_ada.shape[0]
    assert d == D_MODEL and seq % TM_MM == 0
    xt = x.reshape(batch * seq, d)
    mod = _ada(c, w_ada, b_ada)
    normed = False
    for l in range(depth):
        base = l * batch * 6
        h = _norm_mod(xt, g_mix[l], mod, base, 0, 1, seq)
        u = _matmul(h, w_in[l].astype(BF16), TN_IN, BF16)
        ca = _conv_branch(u, conv_w[l], conv_b[l], conv_ln_g[l], conv_ln_b[l], batch, seq)
        att = [_attn_group(u, gi, batch, seq) for gi in range(ATT_GROUPS)]
        ob = _attn_combine([a[0] for a in att], [a[1] for a in att])
        rc = _retention_branch(u, batch, seq)
        merged = _merge(ca, ob, rc, w_proj_a[l].astype(BF16), w_proj_b[l].astype(BF16),
                        w_proj_c[l].astype(BF16), u)
        xt = _matmul_residual(merged, w_out[l].astype(BF16), xt, mod, base, 2, seq)
        j = l // 2
        if l % 2 == 0:
            xt = _dense_ffn(xt, g_ffn[l], mod, base, w_ff_gate[j].astype(BF16), w_ff_up[j].astype(BF16),
                            w_ff_down[j].astype(BF16), seq)
        else:
            normed = l == depth - 1
            xt = _moe_layer(xt, g_ffn[l], mod, base, w_router[j], b_router[j], w_exp_gate[j].astype(BF16),
                            w_exp_up[j].astype(BF16), w_exp_down[j].astype(BF16), g_final, seq, normed)
    if not normed:
        xt = _final_norm(xt, g_final)
    return xt.reshape(batch, seq, d)
```

```python
import functools
import math

import numpy as np
import jax
import jax.numpy as jnp
from jax import lax
from jax.experimental import pallas as pl
from jax.experimental.pallas import tpu as pltpu

F32 = jnp.float32
BF16 = jnp.bfloat16

D_MODEL = 2048
CONV_DIM = 1536
CONV_WIDTH = 31
ATT_HEAD_DIM = 128
ATT_HEADS_PER_GROUP = 4
ATT_PATTERNS = ((128, 1), (512, 4), (2048, 16))
ATT_GROUPS = len(ATT_PATTERNS)
ATT_HEADS = ATT_GROUPS * ATT_HEADS_PER_GROUP
ATT_OUT_DIM = ATT_HEADS_PER_GROUP * ATT_HEAD_DIM
ATT_BLOCK = 128
RET_HEADS = 8
RET_QK_DIM = 128
RET_V_DIM = 256
RET_CHUNK = 128
N_BRANCHES = 3
N_EXPERTS = 8
TOP_K = 2
EPS = 1e-6
COLS_A = 2 * CONV_DIM
COLS_B = 3 * ATT_HEADS * ATT_HEAD_DIM
COLS_QK = RET_HEADS * RET_QK_DIM
COLS_V = RET_HEADS * RET_V_DIM
COLS_GATE = N_BRANCHES * D_MODEL
IN_COLS = COLS_A + COLS_B + 2 * COLS_QK + 2 * COLS_V + COLS_GATE
W_OFF_B = COLS_A
OFF_Q = COLS_A
OFF_K = OFF_Q + COLS_QK
OFF_V = OFF_K + COLS_QK
OFF_G = OFF_V + COLS_V
OFF_GATES = OFF_G + COLS_V
U_COLS = OFF_GATES + COLS_GATE

V7X_VMEM_BYTES = 64 * 1024 * 1024
V7X_LANES = 128
V7X_SUBLANES = 8
VMEM_LIMIT = V7X_VMEM_BYTES * 7 // 8

COL_BLK = 512
TM_NORM = 512
TM_MM = 1024
TN_IN = 768
ATT_SPAN = ATT_BLOCK * max(d for _, d in ATT_PATTERNS)
TM_FF = 512
TF_FF = 512
TF_EXP = 1024
TM_EXP = 512
TS_CONV = 256
CONV_HALO = 32
CONV_ROWS = 32
CONV_LANES = 256
TS_MOVE = 512
NEG = -1e30


def _params(sem):
    return pltpu.CompilerParams(dimension_semantics=sem, vmem_limit_bytes=VMEM_LIMIT)


def _silu(v):
    return v * jax.nn.sigmoid(v)


def _ada_kernel(c_ref, w_ref, b_ref, o_ref):
    cond = _silu(c_ref[...])
    o_ref[0] = jnp.dot(cond.astype(BF16), w_ref[0].astype(BF16), preferred_element_type=F32) + b_ref[0]


def _ada(c, w_ada, b_ada):
    depth, d, n = w_ada.shape
    b = c.shape[0]
    assert b <= V7X_SUBLANES
    c8 = jnp.zeros((V7X_SUBLANES, d), F32).at[:b].set(c)
    tn = 1536
    out = pl.pallas_call(
        _ada_kernel,
        grid=(depth, n // tn),
        in_specs=[pl.BlockSpec((V7X_SUBLANES, d), lambda l, j: (0, 0)),
                  pl.BlockSpec((1, d, tn), lambda l, j: (l, 0, j)),
                  pl.BlockSpec((1, 1, tn), lambda l, j: (l, 0, j))],
        out_specs=pl.BlockSpec((1, V7X_SUBLANES, tn), lambda l, j: (l, 0, j)),
        out_shape=jax.ShapeDtypeStruct((depth, V7X_SUBLANES, n), F32),
        compiler_params=_params(("arbitrary", "arbitrary")),
        name="ada",
    )(c8, w_ada, b_ada.reshape(depth, 1, n))
    return out[:, :b, :].reshape(depth * b * 6, 1, d)


def _mod_spec(base, which, tiles_per_batch, width=D_MODEL):
    return pl.BlockSpec((1, 1, width), lambda i, *_: (base + (i // tiles_per_batch) * 6 + which, 0, 0))


def _norm_mod_value(x, g, sc, sh):
    ms = jnp.mean(x * x, axis=-1, keepdims=True)
    return (x * lax.rsqrt(ms + EPS) * g) * (1.0 + sc) + sh


def _norm_mod_kernel(x_ref, g_ref, sc_ref, sh_ref, o_ref):
    o_ref[...] = _norm_mod_value(x_ref[...], g_ref[...], sc_ref[0], sh_ref[0]).astype(o_ref.dtype)


def _norm_mod(x, g, mod, base, which_shift, which_scale, seq):
    t, d = x.shape
    tpb = seq // TM_NORM
    return pl.pallas_call(
        _norm_mod_kernel,
        grid=(t // TM_NORM,),
        in_specs=[pl.BlockSpec((TM_NORM, d), lambda i: (i, 0)),
                  pl.BlockSpec((1, d), lambda i: (0, 0)),
                  _mod_spec(base, which_scale, tpb),
                  _mod_spec(base, which_shift, tpb)],
        out_specs=pl.BlockSpec((TM_NORM, d), lambda i: (i, 0)),
        out_shape=jax.ShapeDtypeStruct((t, d), BF16),
        compiler_params=_params(("arbitrary",)),
        name="norm_mod",
    )(x, g.reshape(1, d), mod, mod)


def _in_proj_kernel(a_ref, w_ref, o_ref, wb_ref):
    @pl.when(pl.program_id(1) == 0)
    def _():
        wb_ref[...] = w_ref[...].astype(BF16)

    o_ref[...] = jnp.dot(a_ref[...], wb_ref[...], preferred_element_type=F32).astype(o_ref.dtype)


def _in_proj(a, w_in, layer, attention, out_dtype):
    m, k = a.shape
    first, count = W_OFF_B // TN_IN, COLS_B // TN_IN
    if attention:
        n, wcol = COLS_B, lambda j: first + j
    else:
        n, wcol = U_COLS, lambda j: jnp.where(j < first, j, j + count)
    return pl.pallas_call(
        _in_proj_kernel,
        grid=(n // TN_IN, m // TM_MM),
        in_specs=[pl.BlockSpec((TM_MM, k), lambda j, i: (i, 0)),
                  pl.BlockSpec((None, k, TN_IN), lambda j, i: (layer, 0, wcol(j)))],
        out_specs=pl.BlockSpec((TM_MM, TN_IN), lambda j, i: (i, j)),
        out_shape=jax.ShapeDtypeStruct((m, n), out_dtype),
        scratch_shapes=[pltpu.VMEM((k, TN_IN), BF16)],
        compiler_params=_params(("arbitrary", "arbitrary")),
        name="in_proj_att" if attention else "in_proj",
    )(a, w_in)


def _conv_kernel(a_ref, b_ref, w_ref, cb_ref, lg_ref, lb_ref, o_ref, zs_ref, y_ref):
    ts = a_ref.shape[0]
    c = a_ref.shape[1]

    @pl.when(pl.program_id(1) == 0)
    def _():
        zs_ref[0, 0:CONV_HALO, :] = jnp.zeros((CONV_HALO, c), F32)

    zs_ref[0, CONV_HALO:CONV_HALO + ts, :] = a_ref[...].astype(F32) * jax.nn.sigmoid(b_ref[...].astype(F32))
    span = ts + CONV_HALO - V7X_SUBLANES
    for r in range(1, V7X_SUBLANES):
        zs_ref[r, 0:span, :] = zs_ref[0, r:r + span, :]

    first = CONV_HALO - (CONV_WIDTH - 1)

    def rows(i, carry):
        r0 = pl.multiple_of(i * CONV_ROWS, CONV_ROWS)
        for lc in range(c // CONV_LANES):
            ls = slice(lc * CONV_LANES, (lc + 1) * CONV_LANES)
            acc = jnp.broadcast_to(cb_ref[:, ls], (CONV_ROWS, CONV_LANES))
            for k in range(CONV_WIDTH):
                off = first + k
                acc = acc + w_ref[k:k + 1, ls] * zs_ref[off % V7X_SUBLANES,
                                                         pl.ds(r0 + (off // V7X_SUBLANES) * V7X_SUBLANES, CONV_ROWS), ls]
            y_ref[pl.ds(r0, CONV_ROWS), ls] = acc
        return carry

    lax.fori_loop(0, ts // CONV_ROWS, rows, 0)

    zs_ref[0, 0:CONV_HALO, :] = zs_ref[0, ts:ts + CONV_HALO, :]

    y = y_ref[...]
    mu = jnp.mean(y, axis=-1, keepdims=True)
    yc = y - mu
    var = jnp.mean(yc * yc, axis=-1, keepdims=True)
    zn = yc * lax.rsqrt(var + EPS) * lg_ref[...] + lb_ref[...]
    o_ref[...] = _silu(zn).astype(o_ref.dtype)


def _conv_branch(u, conv_w, conv_b, ln_g, ln_b, batch, seq):
    t = u.shape[0]
    c = CONV_DIM
    spb = seq // TS_CONV
    w = jnp.zeros((CONV_HALO, c), F32).at[:CONV_WIDTH].set(conv_w)
    return pl.pallas_call(
        _conv_kernel,
        grid=(batch, spb),
        in_specs=[pl.BlockSpec((TS_CONV, c), lambda b, s: (b * spb + s, 0)),
                  pl.BlockSpec((TS_CONV, c), lambda b, s: (b * spb + s, 1)),
                  pl.BlockSpec((CONV_HALO, c), lambda b, s: (0, 0)),
                  pl.BlockSpec((1, c), lambda b, s: (0, 0)),
                  pl.BlockSpec((1, c), lambda b, s: (0, 0)),
                  pl.BlockSpec((1, c), lambda b, s: (0, 0))],
        out_specs=pl.BlockSpec((TS_CONV, c), lambda b, s: (b * spb + s, 0)),
        out_shape=jax.ShapeDtypeStruct((t, c), BF16),
        scratch_shapes=[pltpu.VMEM((V7X_SUBLANES, TS_CONV + CONV_HALO, c), F32),
                        pltpu.VMEM((TS_CONV, c), F32)],
        compiler_params=_params(("arbitrary", "arbitrary")),
        name="conv_branch",
    )(u, u, w, conv_b.reshape(1, c), ln_g.reshape(1, c), ln_b.reshape(1, c))


def _rows(start, dilation):
    return pl.ds(start, ATT_BLOCK) if dilation == 1 else pl.ds(start, ATT_BLOCK, stride=dilation)


def _attn_kernel(slope_ref, *refs):
    in_refs, o_ref, og_ref, lg_ref = refs[:5 * ATT_GROUPS], refs[5 * ATT_GROUPS], refs[-2], refs[-1]
    span_idx = pl.program_id(1)
    head = pl.program_id(2)
    blk = ATT_BLOCK
    qi = lax.broadcasted_iota(jnp.int32, (blk, 2 * blk), 0)
    ki = lax.broadcasted_iota(jnp.int32, (blk, 2 * blk), 1)
    step = qi + blk - ki
    stepf = step.astype(F32)
    scale = ATT_HEAD_DIM ** -0.5
    for gi, (_, dilation) in enumerate(ATT_PATTERNS):
        q_ref, kc_ref, kp_ref, vc_ref, vp_ref = in_refs[5 * gi:5 * gi + 5]
        per_span = ATT_SPAN // (blk * dilation)
        slope = slope_ref[gi * ATT_HEADS_PER_GROUP + head] * dilation
        first_limit = jnp.where(ki >= blk, blk, jnp.where(span_idx > 0, blk, 0))
        bias_inner = jnp.where(jnp.logical_and(step >= 0, step <= blk), -slope * stepf, NEG)
        bias_first = jnp.where(jnp.logical_and(step >= 0, step <= first_limit), -slope * stepf, NEG)
        for rho in range(dilation):
            for n in range(per_span):
                cur = _rows(n * blk * dilation + rho, dilation)
                if n > 0:
                    prev = _rows((n - 1) * blk * dilation + rho, dilation)
                    k_prev, v_prev, bias = kc_ref[prev, :], vc_ref[prev, :], bias_inner
                else:
                    prev = _rows(rho, dilation)
                    k_prev, v_prev, bias = kp_ref[prev, :], vp_ref[prev, :], bias_first
                q = q_ref[cur, :].astype(BF16)
                k = jnp.concatenate([k_prev, kc_ref[cur, :]], axis=0).astype(BF16)
                v = jnp.concatenate([v_prev, vc_ref[cur, :]], axis=0).astype(BF16)
                s = lax.dot_general(q, k, (((1,), (1,)), ((), ())), preferred_element_type=F32) * scale + bias
                m = jnp.max(s, axis=-1, keepdims=True)
                p = jnp.exp(s - m)
                denom = jnp.sum(p, axis=-1, keepdims=True)
                og_ref[gi, cur, :] = jnp.dot(p.astype(BF16), v, preferred_element_type=F32) / denom
                lg_ref[gi, cur, :] = jnp.broadcast_to(m + jnp.log(denom), (blk, ATT_HEAD_DIM))
    lses = [lg_ref[gi] for gi in range(ATT_GROUPS)]
    top = functools.reduce(jnp.maximum, lses)
    es = [jnp.exp(l - top) for l in lses]
    num = functools.reduce(jnp.add, [e * og_ref[gi] for gi, e in enumerate(es)])
    o_ref[...] = (num / functools.reduce(jnp.add, es)).astype(o_ref.dtype)


def _attn_branch(att, batch, seq):
    t = att.shape[0]
    assert seq % ATT_SPAN == 0
    spans = seq // ATT_SPAN
    hd = ATT_HEAD_DIM
    slopes = jnp.asarray(2.0 ** (-8.0 * np.arange(1, ATT_HEADS + 1) / ATT_HEADS), F32)
    in_specs = []
    for gi, (window, dilation) in enumerate(ATT_PATTERNS):
        assert window // dilation == ATT_BLOCK and ATT_SPAN % (ATT_BLOCK * dilation) == 0
        reach = ATT_BLOCK * dilation
        per_span = ATT_SPAN // reach

        def cur(which, gi=gi):
            col = (gi * 3 + which) * ATT_HEADS_PER_GROUP
            return pl.BlockSpec((ATT_SPAN, hd), lambda b, s, h, sl: (b * spans + s, col + h))

        def prev(which, gi=gi, per_span=per_span, reach=reach):
            col = (gi * 3 + which) * ATT_HEADS_PER_GROUP
            return pl.BlockSpec((reach, hd),
                                lambda b, s, h, sl: (jnp.maximum((b * spans + s) * per_span - 1, 0), col + h))

        in_specs += [cur(0), cur(1), prev(1), cur(2), prev(2)]
    return pl.pallas_call(
        _attn_kernel,
        grid_spec=pltpu.PrefetchScalarGridSpec(
            num_scalar_prefetch=1,
            grid=(batch, spans, ATT_HEADS_PER_GROUP),
            in_specs=in_specs,
            out_specs=pl.BlockSpec((ATT_SPAN, hd), lambda b, s, h, sl: (b * spans + s, h)),
            scratch_shapes=[pltpu.VMEM((ATT_GROUPS, ATT_SPAN, hd), F32),
                            pltpu.VMEM((ATT_GROUPS, ATT_SPAN, hd), F32)]),
        out_shape=jax.ShapeDtypeStruct((t, ATT_OUT_DIM), BF16),
        compiler_params=_params(("arbitrary", "arbitrary", "arbitrary")),
        name="attention",
    )(slopes, *([att] * (5 * ATT_GROUPS)))


RET_HEADS_PER_STEP = 4


def _retention_tables():
    h = np.arange(RET_HEADS, dtype=np.float64)
    log_gamma = np.log1p(-(2.0 ** (-5.0 - h)))
    idx = np.arange(RET_CHUNK, dtype=np.float64)
    diff = idx[:, None] - idx[None, :]
    scale = RET_QK_DIM ** -0.5
    intra = np.where(diff >= 0, np.exp(log_gamma[:, None, None] * np.maximum(diff, 0.0)), 0.0) * scale
    q_decay = np.exp(log_gamma[:, None] * (idx + 1.0))
    k_decay = np.exp(log_gamma[:, None] * (RET_CHUNK - 1.0 - idx)) * scale
    chunk_decay = np.exp(log_gamma * RET_CHUNK)
    qd = np.broadcast_to(q_decay[:, :, None], (RET_HEADS, RET_CHUNK, RET_QK_DIM))
    kd = np.broadcast_to(k_decay[:, :, None], (RET_HEADS, RET_CHUNK, RET_QK_DIM))
    cd = np.broadcast_to(chunk_decay[:, None, None], (RET_HEADS, 1, RET_V_DIM))
    return tuple(jnp.asarray(a, F32) for a in (intra, qd, kd, cd))


def _retention_kernel(q_ref, k_ref, va_ref, vb_ref, ga_ref, gb_ref, intra_ref, qd_ref, kd_ref, cd_ref,
                      o_ref, st_ref):
    @pl.when(pl.program_id(2) == 0)
    def _():
        st_ref[...] = jnp.zeros(st_ref.shape, F32)

    half = RET_HEADS_PER_STEP // 2
    for hh in range(RET_HEADS_PER_STEP):
        qs = slice(hh * RET_QK_DIM, (hh + 1) * RET_QK_DIM)
        vs = slice((hh % half) * RET_V_DIM, (hh % half + 1) * RET_V_DIM)
        q = q_ref[:, qs]
        k = k_ref[:, qs]
        v = (va_ref if hh < half else vb_ref)[:, vs]
        g = (ga_ref if hh < half else gb_ref)[:, vs].astype(F32)
        s = lax.dot_general(q, k, (((1,), (1,)), ((), ())), preferred_element_type=F32) * intra_ref[hh]
        q_dec = (q.astype(F32) * qd_ref[hh]).astype(BF16)
        st = st_ref[hh]
        o = (jnp.dot(s.astype(BF16), v, preferred_element_type=F32)
             + jnp.dot(q_dec, st.astype(BF16), preferred_element_type=F32))
        k_dec = (k.astype(F32) * kd_ref[hh]).T.astype(BF16)
        st_ref[hh] = st * cd_ref[hh] + jnp.dot(k_dec, v, preferred_element_type=F32)
        on = o * lax.rsqrt(jnp.mean(o * o, axis=-1, keepdims=True) + EPS)
        o_ref[:, hh * RET_V_DIM:(hh + 1) * RET_V_DIM] = (_silu(g) * on).astype(o_ref.dtype)


def _retention_branch(u, batch, seq):
    t = u.shape[0]
    hps = RET_HEADS_PER_STEP
    assert hps * RET_QK_DIM == COL_BLK and (hps // 2) * RET_V_DIM == COL_BLK
    nchunk = seq // RET_CHUNK
    groups = RET_HEADS // hps

    def col(off, mult, add):
        return pl.BlockSpec((RET_CHUNK, COL_BLK), lambda b, hg, c: (b * nchunk + c, off // COL_BLK + mult * hg + add))

    def table(shape):
        return pl.BlockSpec((hps,) + shape, lambda b, hg, c: (hg, 0, 0))

    return pl.pallas_call(
        _retention_kernel,
        grid=(batch, groups, nchunk),
        in_specs=[col(OFF_Q, 1, 0), col(OFF_K, 1, 0), col(OFF_V, 2, 0), col(OFF_V, 2, 1),
                  col(OFF_G, 2, 0), col(OFF_G, 2, 1),
                  table((RET_CHUNK, RET_CHUNK)), table((RET_CHUNK, RET_QK_DIM)), table((RET_CHUNK, RET_QK_DIM)),
                  table((1, RET_V_DIM))],
        out_specs=pl.BlockSpec((RET_CHUNK, hps * RET_V_DIM), lambda b, hg, c: (b * nchunk + c, hg)),
        out_shape=jax.ShapeDtypeStruct((t, COLS_V), BF16),
        scratch_shapes=[pltpu.VMEM((hps, RET_QK_DIM, RET_V_DIM), F32)],
        compiler_params=_params(("arbitrary", "arbitrary", "arbitrary")),
        name="retention",
    )(u, u, u, u, u, u, *_retention_tables())


def _merge_kernel(ca_ref, ob_ref, rc_ref, wa_ref, wb_ref, wc_ref, g0_ref, g1_ref, g2_ref, o_ref):
    acc = None
    for x_ref, w_ref, g_ref in ((ca_ref, wa_ref, g0_ref), (ob_ref, wb_ref, g1_ref), (rc_ref, wc_ref, g2_ref)):
        y = jnp.dot(x_ref[...], w_ref[...], preferred_element_type=F32)
        term = jax.nn.sigmoid(g_ref[...].astype(F32)) * y
        acc = term if acc is None else acc + term
    o_ref[...] = acc.astype(o_ref.dtype)


def _merge(ca, ob, rc, wa, wb, wc, u):
    t = ca.shape[0]
    d = wa.shape[1]
    tn = COL_BLK
    gate_blk = OFF_GATES // tn
    per_gate = d // tn

    def lhs(a):
        return pl.BlockSpec((TM_MM, a.shape[1]), lambda i, j: (i, 0))

    def rhs(w):
        return pl.BlockSpec((w.shape[0], tn), lambda i, j: (0, j))

    def gate(k):
        return pl.BlockSpec((TM_MM, tn), lambda i, j: (i, gate_blk + k * per_gate + j))

    return pl.pallas_call(
        _merge_kernel,
        grid=(t // TM_MM, d // tn),
        in_specs=[lhs(ca), lhs(ob), lhs(rc), rhs(wa), rhs(wb), rhs(wc), gate(0), gate(1), gate(2)],
        out_specs=pl.BlockSpec((TM_MM, tn), lambda i, j: (i, j)),
        out_shape=jax.ShapeDtypeStruct((t, d), BF16),
        compiler_params=_params(("arbitrary", "arbitrary")),
        name="merge",
    )(ca, ob, rc, wa, wb, wc, u, u, u)


def _matmul_residual_kernel(a_ref, w_ref, x_ref, gt_ref, o_ref):
    y = jnp.dot(a_ref[...], w_ref[...], preferred_element_type=F32)
    o_ref[...] = x_ref[...] + gt_ref[0] * y


def _matmul_residual(a, w, x, mod, base, which, seq):
    t, k = a.shape
    d = w.shape[1]
    tn = COL_BLK
    tpb = seq // TM_MM
    return pl.pallas_call(
        _matmul_residual_kernel,
        grid=(t // TM_MM, d // tn),
        in_specs=[pl.BlockSpec((TM_MM, k), lambda i, j: (i, 0)),
                  pl.BlockSpec((k, tn), lambda i, j: (0, j)),
                  pl.BlockSpec((TM_MM, tn), lambda i, j: (i, j)),
                  pl.BlockSpec((1, 1, tn), lambda i, j: (base + (i // tpb) * 6 + which, 0, j))],
        out_specs=pl.BlockSpec((TM_MM, tn), lambda i, j: (i, j)),
        out_shape=jax.ShapeDtypeStruct((t, d), F32),
        compiler_params=_params(("arbitrary", "arbitrary")),
        name="out_proj",
    )(a, w, x, mod)


def _swiglu_step(h, wg_ref, wu_ref, wd_ref):
    a = jnp.dot(h, wg_ref[...], preferred_element_type=F32)
    b = jnp.dot(h, wu_ref[...], preferred_element_type=F32)
    return jnp.dot((_silu(a) * b).astype(BF16), wd_ref[...], preferred_element_type=F32)


def _dense_ffn_kernel(x_ref, g_ref, sc_ref, sh_ref, gt_ref, wg_ref, wu_ref, wd_ref, o_ref, h_ref):
    j = pl.program_id(1)

    @pl.when(j == 0)
    def _():
        h_ref[...] = _norm_mod_value(x_ref[...], g_ref[...], sc_ref[0], sh_ref[0]).astype(BF16)

    part = _swiglu_step(h_ref[...], wg_ref, wu_ref, wd_ref)

    @pl.when(j == 0)
    def _():
        o_ref[...] = part

    @pl.when(j > 0)
    def _():
        o_ref[...] += part

    @pl.when(j == pl.num_programs(1) - 1)
    def _():
        o_ref[...] = x_ref[...] + gt_ref[0] * o_ref[...]


def _dense_ffn(x, g, mod, base, wg, wu, wd, seq):
    t, d = x.shape
    f = wg.shape[1]
    tpb = seq // TM_FF
    row = pl.BlockSpec((TM_FF, d), lambda i, j: (i, 0))
    return pl.pallas_call(
        _dense_ffn_kernel,
        grid=(t // TM_FF, f // TF_FF),
        in_specs=[row,
                  pl.BlockSpec((1, d), lambda i, j: (0, 0)),
                  _mod_spec(base, 4, tpb), _mod_spec(base, 3, tpb), _mod_spec(base, 5, tpb),
                  pl.BlockSpec((d, TF_FF), lambda i, j: (0, j)),
                  pl.BlockSpec((d, TF_FF), lambda i, j: (0, j)),
                  pl.BlockSpec((TF_FF, d), lambda i, j: (j, 0))],
        out_specs=row,
        out_shape=jax.ShapeDtypeStruct((t, d), F32),
        scratch_shapes=[pltpu.VMEM((TM_FF, d), BF16)],
        compiler_params=_params(("arbitrary", "arbitrary")),
        name="dense_ffn",
    )(x, g.reshape(1, d), mod, mod, mod, wg, wu, wd)


META_E1, META_E2, META_R1, META_R2, META_W1, META_W2 = range(6)


def _router_kernel(x_ref, g_ref, sc_ref, sh_ref, wr_ref, br_ref, h_ref, meta_ref, cnt_ref):
    tm = x_ref.shape[0]

    @pl.when(pl.program_id(0) == 0)
    def _():
        cnt_ref[...] = jnp.zeros(cnt_ref.shape, F32)

    h = _norm_mod_value(x_ref[...], g_ref[...], sc_ref[0], sh_ref[0])
    h_ref[...] = h
    lane = lax.broadcasted_iota(jnp.int32, (tm, V7X_LANES), 1).astype(F32)
    logits = jnp.dot(h.astype(BF16), wr_ref[...], preferred_element_type=F32) + br_ref[...]
    logits = jnp.where(lane < N_EXPERTS, logits, NEG)
    m1 = jnp.max(logits, axis=-1, keepdims=True)
    i1 = jnp.min(jnp.where(logits == m1, lane, float(V7X_LANES)), axis=-1, keepdims=True)
    rest = jnp.where(lane == i1, NEG, logits)
    m2 = jnp.max(rest, axis=-1, keepdims=True)
    i2 = jnp.min(jnp.where(rest == m2, lane, float(V7X_LANES)), axis=-1, keepdims=True)
    e21 = jnp.exp(m2 - m1)
    w1 = 1.0 / (1.0 + e21)
    w2 = e21 * w1
    hot1 = lane == i1
    hot2 = lane == i2
    hot = jnp.where(jnp.logical_or(hot1, hot2), 1.0, 0.0)
    ri = lax.broadcasted_iota(jnp.int32, (tm, tm), 0)
    ci = lax.broadcasted_iota(jnp.int32, (tm, tm), 1)
    earlier = jnp.where(ci < ri, 1.0, 0.0).astype(BF16)
    before = jnp.dot(earlier, hot.astype(BF16), preferred_element_type=F32) + cnt_ref[...]
    r1 = jnp.sum(jnp.where(hot1, before, 0.0), axis=-1, keepdims=True)
    r2 = jnp.sum(jnp.where(hot2, before, 0.0), axis=-1, keepdims=True)
    cnt_ref[...] += jnp.sum(hot, axis=0, keepdims=True)
    meta = jnp.zeros((tm, V7X_LANES), F32)
    for slot, val in ((META_E1, i1), (META_E2, i2), (META_R1, r1), (META_R2, r2),
                      (META_W1, w1), (META_W2, w2)):
        meta = jnp.where(lane == slot, val, meta)
    meta_ref[...] = meta


def _router(x, g, mod, base, w_router, b_router, seq):
    t, d = x.shape
    tpb = seq // TM_NORM
    wr = jnp.zeros((d, V7X_LANES), BF16).at[:, :N_EXPERTS].set(w_router.astype(BF16))
    br = jnp.zeros((1, V7X_LANES), F32).at[0, :N_EXPERTS].set(b_router)
    return pl.pallas_call(
        _router_kernel,
        grid=(t // TM_NORM,),
        in_specs=[pl.BlockSpec((TM_NORM, d), lambda i: (i, 0)),
                  pl.BlockSpec((1, d), lambda i: (0, 0)),
                  _mod_spec(base, 4, tpb), _mod_spec(base, 3, tpb),
                  pl.BlockSpec((d, V7X_LANES), lambda i: (0, 0)),
                  pl.BlockSpec((1, V7X_LANES), lambda i: (0, 0))],
        out_specs=[pl.BlockSpec((TM_NORM, d), lambda i: (i, 0)),
                   pl.BlockSpec((TM_NORM, V7X_LANES), lambda i: (i, 0)),
                   pl.BlockSpec((1, V7X_LANES), lambda i: (0, 0))],
        out_shape=[jax.ShapeDtypeStruct((t, d), F32),
                   jax.ShapeDtypeStruct((t, V7X_LANES), F32),
                   jax.ShapeDtypeStruct((1, V7X_LANES), F32)],
        compiler_params=_params(("arbitrary",)),
        name="router",
    )(x, g.reshape(1, d), mod, mod, wr, br)


def _scatter_kernel(pos_ref, h_ref, xs_in_ref, xs_ref, sem):
    del xs_in_ref
    ts = h_ref.shape[0]
    t_total = pl.num_programs(0) * ts
    base = pl.program_id(0) * ts

    def copy(r, k):
        dst = pos_ref[k * t_total + base + r]
        return pltpu.make_async_copy(h_ref.at[pl.ds(r, 1)], xs_ref.at[pl.ds(dst, 1)], sem)

    def start(r, carry):
        copy(r, 0).start()
        copy(r, 1).start()
        return carry

    def wait(r, carry):
        copy(r, 0).wait()
        copy(r, 1).wait()
        return carry

    lax.fori_loop(0, ts, start, 0)
    lax.fori_loop(0, ts, wait, 0)


def _scatter_rows(h, pos, rows):
    t, d = h.shape
    return pl.pallas_call(
        _scatter_kernel,
        grid_spec=pltpu.PrefetchScalarGridSpec(
            num_scalar_prefetch=1,
            grid=(t // TS_MOVE,),
            in_specs=[pl.BlockSpec((TS_MOVE, d), lambda i, pos: (i, 0)),
                      pl.BlockSpec(memory_space=pl.ANY)],
            out_specs=pl.BlockSpec(memory_space=pl.ANY),
            scratch_shapes=[pltpu.SemaphoreType.DMA(())]),
        out_shape=jax.ShapeDtypeStruct((rows, d), F32),
        input_output_aliases={2: 0},
        compiler_params=_params(("arbitrary",)),
        name="expert_scatter",
    )(pos, h, jnp.zeros((rows, d), F32))


def _expert_ffn_kernel(te_ref, na_ref, xs_ref, wg_ref, wu_ref, wd_ref, o_ref, h_ref):
    i = pl.program_id(0)
    j = pl.program_id(1)
    active = i < na_ref[0]

    @pl.when(jnp.logical_and(active, j == 0))
    def _():
        h_ref[...] = xs_ref[...].astype(BF16)

    @pl.when(jnp.logical_and(jnp.logical_not(active), j == 0))
    def _():
        o_ref[...] = jnp.zeros(o_ref.shape, F32)

    @pl.when(active)
    def _():
        part = _swiglu_step(h_ref[...], wg_ref.at[0], wu_ref.at[0], wd_ref.at[0])

        @pl.when(j == 0)
        def _():
            o_ref[...] = part

        @pl.when(j > 0)
        def _():
            o_ref[...] += part


def _expert_ffn(xs, tile_expert, n_active, wg, wu, wd):
    rows, d = xs.shape
    f = wg.shape[2]
    nf = f // TF_EXP

    def jcol(i, j, na):
        return jnp.where(i < na[0], j, nf - 1)

    row = pl.BlockSpec((TM_EXP, d), lambda i, j, te, na: (i, 0))
    return pl.pallas_call(
        _expert_ffn_kernel,
        grid_spec=pltpu.PrefetchScalarGridSpec(
            num_scalar_prefetch=2,
            grid=(rows // TM_EXP, nf),
            in_specs=[row,
                      pl.BlockSpec((1, d, TF_EXP), lambda i, j, te, na: (te[i], 0, jcol(i, j, na))),
                      pl.BlockSpec((1, d, TF_EXP), lambda i, j, te, na: (te[i], 0, jcol(i, j, na))),
                      pl.BlockSpec((1, TF_EXP, d), lambda i, j, te, na: (te[i], jcol(i, j, na), 0))],
            out_specs=row,
            scratch_shapes=[pltpu.VMEM((TM_EXP, d), BF16)]),
        out_shape=jax.ShapeDtypeStruct((rows, d), F32),
        compiler_params=_params(("arbitrary", "arbitrary")),
        name="expert_ffn",
    )(tile_expert, n_active, xs, wg, wu, wd)


def _combine_kernel(pos_ref, x_ref, meta_ref, gt_ref, gf_ref, ys_ref, o_ref, y1_ref, y2_ref, sem, *, final_norm):
    ts = x_ref.shape[0]
    t_total = pl.num_programs(0) * ts
    base = pl.program_id(0) * ts

    def copy(r, k):
        src = pos_ref[k * t_total + base + r]
        dst = y1_ref if k == 0 else y2_ref
        return pltpu.make_async_copy(ys_ref.at[pl.ds(src, 1)], dst.at[pl.ds(r, 1)], sem)

    def start(r, carry):
        copy(r, 0).start()
        copy(r, 1).start()
        return carry

    def wait(r, carry):
        copy(r, 0).wait()
        copy(r, 1).wait()
        return carry

    lax.fori_loop(0, ts, start, 0)
    lax.fori_loop(0, ts, wait, 0)
    meta = meta_ref[...]
    w1 = meta[:, META_W1:META_W1 + 1]
    w2 = meta[:, META_W2:META_W2 + 1]
    f = w1 * y1_ref[...] + w2 * y2_ref[...]
    xn = x_ref[...] + gt_ref[0] * f
    if final_norm:
        xn = xn * lax.rsqrt(jnp.mean(xn * xn, axis=-1, keepdims=True) + EPS) * gf_ref[...]
    o_ref[...] = xn


def _combine(x, meta, pos, ys, mod, base, g_final, seq, final_norm):
    t, d = x.shape
    tpb = seq // TS_MOVE
    row = pl.BlockSpec((TS_MOVE, d), lambda i, pos: (i, 0))
    return pl.pallas_call(
        functools.partial(_combine_kernel, final_norm=final_norm),
        grid_spec=pltpu.PrefetchScalarGridSpec(
            num_scalar_prefetch=1,
            grid=(t // TS_MOVE,),
            in_specs=[row,
                      pl.BlockSpec((TS_MOVE, V7X_LANES), lambda i, pos: (i, 0)),
                      _mod_spec(base, 5, tpb),
                      pl.BlockSpec((1, d), lambda i, pos: (0, 0)),
                      pl.BlockSpec(memory_space=pl.ANY)],
            out_specs=row,
            scratch_shapes=[pltpu.VMEM((TS_MOVE, d), F32), pltpu.VMEM((TS_MOVE, d), F32),
                            pltpu.SemaphoreType.DMA(())]),
        out_shape=jax.ShapeDtypeStruct((t, d), F32),
        compiler_params=_params(("arbitrary",)),
        name="expert_combine",
    )(pos, x, meta, mod, g_final.reshape(1, d), ys)


def _moe_layer(x, g, mod, base, w_router, b_router, wg, wu, wd, g_final, seq, final_norm):
    t, d = x.shape
    h, meta, counts = _router(x, g, mod, base, w_router, b_router, seq)
    counts = counts[0, :N_EXPERTS].astype(jnp.int32)
    tiles = (counts + TM_EXP - 1) // TM_EXP
    tile_end = jnp.cumsum(tiles)
    offsets = (tile_end - tiles) * TM_EXP
    n_tiles = (TOP_K * t) // TM_EXP + N_EXPERTS
    n_active = tile_end[-1:]
    tile_ids = jnp.minimum(jnp.arange(n_tiles, dtype=jnp.int32), n_active[0] - 1)
    tile_expert = jnp.sum(tile_ids[:, None] >= tile_end[None, :], axis=1).astype(jnp.int32)
    e1 = meta[:, META_E1].astype(jnp.int32)
    e2 = meta[:, META_E2].astype(jnp.int32)
    pos = jnp.concatenate([offsets[e1] + meta[:, META_R1].astype(jnp.int32),
                           offsets[e2] + meta[:, META_R2].astype(jnp.int32)])
    xs = _scatter_rows(h, pos, n_tiles * TM_EXP)
    ys = _expert_ffn(xs, tile_expert, n_active.astype(jnp.int32), wg, wu, wd)
    return _combine(x, meta, pos, ys, mod, base, g_final, seq, final_norm)


def _final_norm_kernel(x_ref, g_ref, o_ref):
    x = x_ref[...]
    o_ref[...] = x * lax.rsqrt(jnp.mean(x * x, axis=-1, keepdims=True) + EPS) * g_ref[...]


def _final_norm(x, g):
    t, d = x.shape
    row = pl.BlockSpec((TM_NORM, d), lambda i: (i, 0))
    return pl.pallas_call(
        _final_norm_kernel,
        grid=(t // TM_NORM,),
        in_specs=[row, pl.BlockSpec((1, d), lambda i: (0, 0))],
        out_specs=row,
        out_shape=jax.ShapeDtypeStruct((t, d), F32),
        compiler_params=_params(("arbitrary",)),
        name="final_norm",
    )(x, g.reshape(1, d))


def kernel(x, c, w_ada, b_ada, g_mix, w_in, conv_w, conv_b, conv_ln_g, conv_ln_b, w_proj_a, w_proj_b, w_proj_c,
           w_out, g_ffn, w_ff_gate, w_ff_up, w_ff_down, w_router, b_router, w_exp_gate, w_exp_up, w_exp_down,
           g_final):
    batch, seq, d = x.shape
    depth = w_ada.shape[0]
    assert d == D_MODEL and seq % TM_MM == 0
    xt = x.reshape(batch * seq, d)
    mod = _ada(c, w_ada, b_ada)
    normed = False
    for l in range(depth):
        base = l * batch * 6
        h = _norm_mod(xt, g_mix[l], mod, base, 0, 1, seq)
        u = _in_proj(h, w_in, l, False, BF16)
        att = _in_proj(h, w_in, l, True, F32)
        ca = _conv_branch(u, conv_w[l], conv_b[l], conv_ln_g[l], conv_ln_b[l], batch, seq)
        ob = _attn_branch(att, batch, seq)
        rc = _retention_branch(u, batch, seq)
        merged = _merge(ca, ob, rc, w_proj_a[l].astype(BF16), w_proj_b[l].astype(BF16),
                        w_proj_c[l].astype(BF16), u)
        xt = _matmul_residual(merged, w_out[l].astype(BF16), xt, mod, base, 2, seq)
        j = l // 2
        if l % 2 == 0:
            xt = _dense_ffn(xt, g_ffn[l], mod, base, w_ff_gate[j].astype(BF16), w_ff_up[j].astype(BF16),
                            w_ff_down[j].astype(BF16), seq)
        else:
            normed = l == depth - 1
            xt = _moe_layer(xt, g_ffn[l], mod, base, w_router[j], b_router[j], w_exp_gate[j].astype(BF16),
                            w_exp_up[j].astype(BF16), w_exp_down[j].astype(BF16), g_final, seq, normed)
    if not normed:
        xt = _final_norm(xt, g_final)
    return xt.reshape(batch, seq, d)
```

```python
import functools
import math

import numpy as np
import jax
import jax.numpy as jnp
from jax import lax
from jax.experimental import pallas as pl
from jax.experimental.pallas import tpu as pltpu

F32 = jnp.float32
BF16 = jnp.bfloat16

D_MODEL = 2048
CONV_DIM = 1536
CONV_WIDTH = 31
ATT_HEAD_DIM = 128
ATT_HEADS_PER_GROUP = 4
ATT_PATTERNS = ((128, 1), (512, 4), (2048, 16))
ATT_GROUPS = len(ATT_PATTERNS)
ATT_HEADS = ATT_GROUPS * ATT_HEADS_PER_GROUP
ATT_OUT_DIM = ATT_HEADS_PER_GROUP * ATT_HEAD_DIM
ATT_BLOCK = 128
RET_HEADS = 8
RET_QK_DIM = 128
RET_V_DIM = 256
RET_CHUNK = 128
N_BRANCHES = 3
N_EXPERTS = 8
TOP_K = 2
EPS = 1e-6
COLS_A = 2 * CONV_DIM
COLS_B = 3 * ATT_HEADS * ATT_HEAD_DIM
COLS_QK = RET_HEADS * RET_QK_DIM
COLS_V = RET_HEADS * RET_V_DIM
COLS_GATE = N_BRANCHES * D_MODEL
IN_COLS = COLS_A + COLS_B + 2 * COLS_QK + 2 * COLS_V + COLS_GATE
W_OFF_B = COLS_A
OFF_Q = COLS_A
OFF_K = OFF_Q + COLS_QK
OFF_V = OFF_K + COLS_QK
OFF_G = OFF_V + COLS_V
OFF_GATES = OFF_G + COLS_V
U_COLS = OFF_GATES + COLS_GATE

V7X_VMEM_BYTES = 64 * 1024 * 1024
V7X_LANES = 128
V7X_SUBLANES = 8
VMEM_LIMIT = V7X_VMEM_BYTES * 7 // 8

COL_BLK = 512
TM_NORM = 512
TM_MM = 1024
TM_IN = 2048
TN_IN = 768
ATT_SPAN = ATT_BLOCK * max(d for _, d in ATT_PATTERNS)
TM_FF = 512
TF_FF = 512
TF_EXP = 1024
TM_EXP = 512
TS_CONV = 256
CONV_HALO = 32
CONV_ROWS = 32
CONV_LANES = 256
TS_MOVE = 512
NEG = -1e30


def _params(sem):
    return pltpu.CompilerParams(dimension_semantics=sem, vmem_limit_bytes=VMEM_LIMIT)


def _silu(v):
    return v * jax.nn.sigmoid(v)


def _ada_kernel(c_ref, w_ref, b_ref, o_ref):
    cond = _silu(c_ref[...])
    o_ref[0] = jnp.dot(cond.astype(BF16), w_ref[0].astype(BF16), preferred_element_type=F32) + b_ref[0]


def _ada(c, w_ada, b_ada):
    depth, d, n = w_ada.shape
    b = c.shape[0]
    assert b <= V7X_SUBLANES
    c8 = jnp.zeros((V7X_SUBLANES, d), F32).at[:b].set(c)
    tn = 1536
    out = pl.pallas_call(
        _ada_kernel,
        grid=(depth, n // tn),
        in_specs=[pl.BlockSpec((V7X_SUBLANES, d), lambda l, j: (0, 0)),
                  pl.BlockSpec((1, d, tn), lambda l, j: (l, 0, j)),
                  pl.BlockSpec((1, 1, tn), lambda l, j: (l, 0, j))],
        out_specs=pl.BlockSpec((1, V7X_SUBLANES, tn), lambda l, j: (l, 0, j)),
        out_shape=jax.ShapeDtypeStruct((depth, V7X_SUBLANES, n), F32),
        compiler_params=_params(("arbitrary", "arbitrary")),
        name="ada",
    )(c8, w_ada, b_ada.reshape(depth, 1, n))
    return out[:, :b, :].reshape(depth * b * 6, 1, d)


def _mod_spec(base, which, tiles_per_batch, width=D_MODEL):
    return pl.BlockSpec((1, 1, width), lambda i, *_: (base + (i // tiles_per_batch) * 6 + which, 0, 0))


def _norm_mod_value(x, g, sc, sh):
    ms = jnp.mean(x * x, axis=-1, keepdims=True)
    return (x * lax.rsqrt(ms + EPS) * g) * (1.0 + sc) + sh


def _norm_mod_kernel(x_ref, g_ref, sc_ref, sh_ref, o_ref):
    o_ref[...] = _norm_mod_value(x_ref[...], g_ref[...], sc_ref[0], sh_ref[0]).astype(o_ref.dtype)


def _norm_mod(x, g, mod, base, which_shift, which_scale, seq):
    t, d = x.shape
    tpb = seq // TM_NORM
    return pl.pallas_call(
        _norm_mod_kernel,
        grid=(t // TM_NORM,),
        in_specs=[pl.BlockSpec((TM_NORM, d), lambda i: (i, 0)),
                  pl.BlockSpec((1, d), lambda i: (0, 0)),
                  _mod_spec(base, which_scale, tpb),
                  _mod_spec(base, which_shift, tpb)],
        out_specs=pl.BlockSpec((TM_NORM, d), lambda i: (i, 0)),
        out_shape=jax.ShapeDtypeStruct((t, d), BF16),
        compiler_params=_params(("arbitrary",)),
        name="norm_mod",
    )(x, g.reshape(1, d), mod, mod)


def _in_proj_kernel(a_ref, w_ref, o_ref, wb_ref):
    @pl.when(pl.program_id(1) == 0)
    def _():
        wb_ref[...] = w_ref[...].astype(BF16)

    o_ref[...] = jnp.dot(a_ref[...], wb_ref[...], preferred_element_type=F32).astype(o_ref.dtype)


def _in_proj(a, w_in, layer, attention, out_dtype):
    m, k = a.shape
    first, count = W_OFF_B // TN_IN, COLS_B // TN_IN
    if attention:
        n, wcol = COLS_B, lambda j: first + j
    else:
        n, wcol = U_COLS, lambda j: jnp.where(j < first, j, j + count)
    return pl.pallas_call(
        _in_proj_kernel,
        grid=(n // TN_IN, m // TM_IN),
        in_specs=[pl.BlockSpec((TM_IN, k), lambda j, i: (i, 0)),
                  pl.BlockSpec((None, k, TN_IN), lambda j, i: (layer, 0, wcol(j)))],
        out_specs=pl.BlockSpec((TM_IN, TN_IN), lambda j, i: (i, j)),
        out_shape=jax.ShapeDtypeStruct((m, n), out_dtype),
        scratch_shapes=[pltpu.VMEM((k, TN_IN), BF16)],
        compiler_params=_params(("arbitrary", "arbitrary")),
        name="in_proj_att" if attention else "in_proj",
    )(a, w_in)


def _conv_kernel(a_ref, b_ref, w_ref, cb_ref, lg_ref, lb_ref, o_ref, zs_ref, y_ref):
    ts = a_ref.shape[0]
    c = a_ref.shape[1]

    @pl.when(pl.program_id(1) == 0)
    def _():
        zs_ref[0, 0:CONV_HALO, :] = jnp.zeros((CONV_HALO, c), F32)

    zs_ref[0, CONV_HALO:CONV_HALO + ts, :] = a_ref[...].astype(F32) * jax.nn.sigmoid(b_ref[...].astype(F32))
    span = ts + CONV_HALO - V7X_SUBLANES
    for r in range(1, V7X_SUBLANES):
        zs_ref[r, 0:span, :] = zs_ref[0, r:r + span, :]

    first = CONV_HALO - (CONV_WIDTH - 1)

    groups = CONV_ROWS // V7X_SUBLANES

    def rows(i, carry):
        r0 = pl.multiple_of(i * CONV_ROWS, CONV_ROWS)
        for lc in range(c // CONV_LANES):
            ls = slice(lc * CONV_LANES, (lc + 1) * CONV_LANES)
            acc = jnp.broadcast_to(cb_ref[:, ls][None], (groups, V7X_SUBLANES, CONV_LANES))
            for k in range(CONV_WIDTH):
                off = first + k
                z = zs_ref[off % V7X_SUBLANES, pl.ds(r0 + (off // V7X_SUBLANES) * V7X_SUBLANES, CONV_ROWS), ls]
                acc = acc + w_ref[k, :, ls][None] * z.reshape(groups, V7X_SUBLANES, CONV_LANES)
            y_ref[pl.ds(r0, CONV_ROWS), ls] = acc.reshape(CONV_ROWS, CONV_LANES)
        return carry

    lax.fori_loop(0, ts // CONV_ROWS, rows, 0)

    zs_ref[0, 0:CONV_HALO, :] = zs_ref[0, ts:ts + CONV_HALO, :]

    y = y_ref[...]
    mu = jnp.mean(y, axis=-1, keepdims=True)
    yc = y - mu
    var = jnp.mean(yc * yc, axis=-1, keepdims=True)
    zn = yc * lax.rsqrt(var + EPS) * lg_ref[...] + lb_ref[...]
    o_ref[...] = _silu(zn).astype(o_ref.dtype)


def _conv_branch(u, conv_w, conv_b, ln_g, ln_b, batch, seq):
    t = u.shape[0]
    c = CONV_DIM
    spb = seq // TS_CONV
    w = jnp.broadcast_to(conv_w[:, None, :], (CONV_WIDTH, V7X_SUBLANES, c))
    cb = jnp.broadcast_to(conv_b[None, :], (V7X_SUBLANES, c))
    return pl.pallas_call(
        _conv_kernel,
        grid=(batch, spb),
        in_specs=[pl.BlockSpec((TS_CONV, c), lambda b, s: (b * spb + s, 0)),
                  pl.BlockSpec((TS_CONV, c), lambda b, s: (b * spb + s, 1)),
                  pl.BlockSpec((CONV_WIDTH, V7X_SUBLANES, c), lambda b, s: (0, 0, 0)),
                  pl.BlockSpec((V7X_SUBLANES, c), lambda b, s: (0, 0)),
                  pl.BlockSpec((1, c), lambda b, s: (0, 0)),
                  pl.BlockSpec((1, c), lambda b, s: (0, 0))],
        out_specs=pl.BlockSpec((TS_CONV, c), lambda b, s: (b * spb + s, 0)),
        out_shape=jax.ShapeDtypeStruct((t, c), BF16),
        scratch_shapes=[pltpu.VMEM((V7X_SUBLANES, TS_CONV + CONV_HALO, c), F32),
                        pltpu.VMEM((TS_CONV, c), F32)],
        compiler_params=_params(("arbitrary", "arbitrary")),
        name="conv_branch",
    )(u, u, w, cb, ln_g.reshape(1, c), ln_b.reshape(1, c))


def _rows(start, dilation):
    return pl.ds(start, ATT_BLOCK) if dilation == 1 else pl.ds(start, ATT_BLOCK, stride=dilation)


def _attn_kernel(slope_ref, *refs):
    in_refs, o_ref, og_ref, lg_ref = refs[:5 * ATT_GROUPS], refs[5 * ATT_GROUPS], refs[-2], refs[-1]
    span_idx = pl.program_id(1)
    head = pl.program_id(2)
    blk = ATT_BLOCK
    qi = lax.broadcasted_iota(jnp.int32, (blk, 2 * blk), 0)
    ki = lax.broadcasted_iota(jnp.int32, (blk, 2 * blk), 1)
    step = qi + blk - ki
    stepf = step.astype(F32)
    scale = ATT_HEAD_DIM ** -0.5
    for gi, (_, dilation) in enumerate(ATT_PATTERNS):
        q_ref, kc_ref, kp_ref, vc_ref, vp_ref = in_refs[5 * gi:5 * gi + 5]
        per_span = ATT_SPAN // (blk * dilation)
        slope = slope_ref[gi * ATT_HEADS_PER_GROUP + head] * dilation
        first_limit = jnp.where(ki >= blk, blk, jnp.where(span_idx > 0, blk, 0))
        bias_inner = jnp.where(jnp.logical_and(step >= 0, step <= blk), -slope * stepf, NEG)
        bias_first = jnp.where(jnp.logical_and(step >= 0, step <= first_limit), -slope * stepf, NEG)
        for rho in range(dilation):
            for n in range(per_span):
                cur = _rows(n * blk * dilation + rho, dilation)
                if n > 0:
                    prev = _rows((n - 1) * blk * dilation + rho, dilation)
                    k_prev, v_prev, bias = kc_ref[prev, :], vc_ref[prev, :], bias_inner
                else:
                    prev = _rows(rho, dilation)
                    k_prev, v_prev, bias = kp_ref[prev, :], vp_ref[prev, :], bias_first
                q = q_ref[cur, :].astype(BF16)
                k = jnp.concatenate([k_prev, kc_ref[cur, :]], axis=0).astype(BF16)
                v = jnp.concatenate([v_prev, vc_ref[cur, :]], axis=0).astype(BF16)
                s = lax.dot_general(q, k, (((1,), (1,)), ((), ())), preferred_element_type=F32) * scale + bias
                m = jnp.max(s, axis=-1, keepdims=True)
                p = jnp.exp(s - m)
                denom = jnp.sum(p, axis=-1, keepdims=True)
                og_ref[gi, cur, :] = jnp.dot(p.astype(BF16), v, preferred_element_type=F32) / denom
                lg_ref[gi, cur, :] = jnp.broadcast_to(m + jnp.log(denom), (blk, ATT_HEAD_DIM))
    lses = [lg_ref[gi] for gi in range(ATT_GROUPS)]
    top = functools.reduce(jnp.maximum, lses)
    es = [jnp.exp(l - top) for l in lses]
    num = functools.reduce(jnp.add, [e * og_ref[gi] for gi, e in enumerate(es)])
    o_ref[...] = (num / functools.reduce(jnp.add, es)).astype(o_ref.dtype)


def _attn_branch(att, batch, seq):
    t = att.shape[0]
    assert seq % ATT_SPAN == 0
    spans = seq // ATT_SPAN
    hd = ATT_HEAD_DIM
    slopes = jnp.asarray(2.0 ** (-8.0 * np.arange(1, ATT_HEADS + 1) / ATT_HEADS), F32)
    in_specs = []
    for gi, (window, dilation) in enumerate(ATT_PATTERNS):
        assert window // dilation == ATT_BLOCK and ATT_SPAN % (ATT_BLOCK * dilation) == 0
        reach = ATT_BLOCK * dilation
        per_span = ATT_SPAN // reach

        def cur(which, gi=gi):
            col = (gi * 3 + which) * ATT_HEADS_PER_GROUP
            return pl.BlockSpec((ATT_SPAN, hd), lambda b, s, h, sl: (b * spans + s, col + h))

        def prev(which, gi=gi, per_span=per_span, reach=reach):
            col = (gi * 3 + which) * ATT_HEADS_PER_GROUP
            return pl.BlockSpec((reach, hd),
                                lambda b, s, h, sl: (jnp.maximum((b * spans + s) * per_span - 1, 0), col + h))

        in_specs += [cur(0), cur(1), prev(1), cur(2), prev(2)]
    return pl.pallas_call(
        _attn_kernel,
        grid_spec=pltpu.PrefetchScalarGridSpec(
            num_scalar_prefetch=1,
            grid=(batch, spans, ATT_HEADS_PER_GROUP),
            in_specs=in_specs,
            out_specs=pl.BlockSpec((ATT_SPAN, hd), lambda b, s, h, sl: (b * spans + s, h)),
            scratch_shapes=[pltpu.VMEM((ATT_GROUPS, ATT_SPAN, hd), F32),
                            pltpu.VMEM((ATT_GROUPS, ATT_SPAN, hd), F32)]),
        out_shape=jax.ShapeDtypeStruct((t, ATT_OUT_DIM), BF16),
        compiler_params=_params(("arbitrary", "arbitrary", "arbitrary")),
        name="attention",
    )(slopes, *([att] * (5 * ATT_GROUPS)))


RET_HEADS_PER_STEP = 4


def _retention_tables():
    h = np.arange(RET_HEADS, dtype=np.float64)
    log_gamma = np.log1p(-(2.0 ** (-5.0 - h)))
    idx = np.arange(RET_CHUNK, dtype=np.float64)
    diff = idx[:, None] - idx[None, :]
    scale = RET_QK_DIM ** -0.5
    intra = np.where(diff >= 0, np.exp(log_gamma[:, None, None] * np.maximum(diff, 0.0)), 0.0) * scale
    q_decay = np.exp(log_gamma[:, None] * (idx + 1.0))
    k_decay = np.exp(log_gamma[:, None] * (RET_CHUNK - 1.0 - idx)) * scale
    chunk_decay = np.exp(log_gamma * RET_CHUNK)
    qd = np.broadcast_to(q_decay[:, :, None], (RET_HEADS, RET_CHUNK, RET_QK_DIM))
    kd = np.broadcast_to(k_decay[:, :, None], (RET_HEADS, RET_CHUNK, RET_QK_DIM))
    cd = np.broadcast_to(chunk_decay[:, None, None], (RET_HEADS, 1, RET_V_DIM))
    return tuple(jnp.asarray(a, F32) for a in (intra, qd, kd, cd))


def _retention_kernel(q_ref, k_ref, va_ref, vb_ref, ga_ref, gb_ref, intra_ref, qd_ref, kd_ref, cd_ref,
                      o_ref, st_ref):
    @pl.when(pl.program_id(2) == 0)
    def _():
        st_ref[...] = jnp.zeros(st_ref.shape, F32)

    half = RET_HEADS_PER_STEP // 2
    for hh in range(RET_HEADS_PER_STEP):
        qs = slice(hh * RET_QK_DIM, (hh + 1) * RET_QK_DIM)
        vs = slice((hh % half) * RET_V_DIM, (hh % half + 1) * RET_V_DIM)
        q = q_ref[:, qs]
        k = k_ref[:, qs]
        v = (va_ref if hh < half else vb_ref)[:, vs]
        g = (ga_ref if hh < half else gb_ref)[:, vs].astype(F32)
        s = lax.dot_general(q, k, (((1,), (1,)), ((), ())), preferred_element_type=F32) * intra_ref[hh]
        q_dec = (q.astype(F32) * qd_ref[hh]).astype(BF16)
        st = st_ref[hh]
        o = (jnp.dot(s.astype(BF16), v, preferred_element_type=F32)
             + jnp.dot(q_dec, st.astype(BF16), preferred_element_type=F32))
        k_dec = (k.astype(F32) * kd_ref[hh]).T.astype(BF16)
        st_ref[hh] = st * cd_ref[hh] + jnp.dot(k_dec, v, preferred_element_type=F32)
        on = o * lax.rsqrt(jnp.mean(o * o, axis=-1, keepdims=True) + EPS)
        o_ref[:, hh * RET_V_DIM:(hh + 1) * RET_V_DIM] = (_silu(g) * on).astype(o_ref.dtype)


def _retention_branch(u, batch, seq):
    t = u.shape[0]
    hps = RET_HEADS_PER_STEP
    assert hps * RET_QK_DIM == COL_BLK and (hps // 2) * RET_V_DIM == COL_BLK
    nchunk = seq // RET_CHUNK
    groups = RET_HEADS // hps

    def col(off, mult, add):
        return pl.BlockSpec((RET_CHUNK, COL_BLK), lambda b, hg, c: (b * nchunk + c, off // COL_BLK + mult * hg + add))

    def table(shape):
        return pl.BlockSpec((hps,) + shape, lambda b, hg, c: (hg, 0, 0))

    return pl.pallas_call(
        _retention_kernel,
        grid=(batch, groups, nchunk),
        in_specs=[col(OFF_Q, 1, 0), col(OFF_K, 1, 0), col(OFF_V, 2, 0), col(OFF_V, 2, 1),
                  col(OFF_G, 2, 0), col(OFF_G, 2, 1),
                  table((RET_CHUNK, RET_CHUNK)), table((RET_CHUNK, RET_QK_DIM)), table((RET_CHUNK, RET_QK_DIM)),
                  table((1, RET_V_DIM))],
        out_specs=pl.BlockSpec((RET_CHUNK, hps * RET_V_DIM), lambda b, hg, c: (b * nchunk + c, hg)),
        out_shape=jax.ShapeDtypeStruct((t, COLS_V), BF16),
        scratch_shapes=[pltpu.VMEM((hps, RET_QK_DIM, RET_V_DIM), F32)],
        compiler_params=_params(("arbitrary", "arbitrary", "arbitrary")),
        name="retention",
    )(u, u, u, u, u, u, *_retention_tables())


def _merge_kernel(ca_ref, ob_ref, rc_ref, wa_ref, wb_ref, wc_ref, g0_ref, g1_ref, g2_ref, o_ref):
    acc = None
    for x_ref, w_ref, g_ref in ((ca_ref, wa_ref, g0_ref), (ob_ref, wb_ref, g1_ref), (rc_ref, wc_ref, g2_ref)):
        y = jnp.dot(x_ref[...], w_ref[...], preferred_element_type=F32)
        term = jax.nn.sigmoid(g_ref[...].astype(F32)) * y
        acc = term if acc is None else acc + term
    o_ref[...] = acc.astype(o_ref.dtype)


def _merge(ca, ob, rc, wa, wb, wc, u):
    t = ca.shape[0]
    d = wa.shape[1]
    tn = COL_BLK
    gate_blk = OFF_GATES // tn
    per_gate = d // tn

    def lhs(a):
        return pl.BlockSpec((TM_MM, a.shape[1]), lambda i, j: (i, 0))

    def rhs(w):
        return pl.BlockSpec((w.shape[0], tn), lambda i, j: (0, j))

    def gate(k):
        return pl.BlockSpec((TM_MM, tn), lambda i, j: (i, gate_blk + k * per_gate + j))

    return pl.pallas_call(
        _merge_kernel,
        grid=(t // TM_MM, d // tn),
        in_specs=[lhs(ca), lhs(ob), lhs(rc), rhs(wa), rhs(wb), rhs(wc), gate(0), gate(1), gate(2)],
        out_specs=pl.BlockSpec((TM_MM, tn), lambda i, j: (i, j)),
        out_shape=jax.ShapeDtypeStruct((t, d), BF16),
        compiler_params=_params(("arbitrary", "arbitrary")),
        name="merge",
    )(ca, ob, rc, wa, wb, wc, u, u, u)


def _matmul_residual_kernel(a_ref, w_ref, x_ref, gt_ref, o_ref):
    y = jnp.dot(a_ref[...], w_ref[...], preferred_element_type=F32)
    o_ref[...] = x_ref[...] + gt_ref[0] * y


def _matmul_residual(a, w, x, mod, base, which, seq):
    t, k = a.shape
    d = w.shape[1]
    tn = COL_BLK
    tpb = seq // TM_MM
    return pl.pallas_call(
        _matmul_residual_kernel,
        grid=(t // TM_MM, d // tn),
        in_specs=[pl.BlockSpec((TM_MM, k), lambda i, j: (i, 0)),
                  pl.BlockSpec((k, tn), lambda i, j: (0, j)),
                  pl.BlockSpec((TM_MM, tn), lambda i, j: (i, j)),
                  pl.BlockSpec((1, 1, tn), lambda i, j: (base + (i // tpb) * 6 + which, 0, j))],
        out_specs=pl.BlockSpec((TM_MM, tn), lambda i, j: (i, j)),
        out_shape=jax.ShapeDtypeStruct((t, d), F32),
        compiler_params=_params(("arbitrary", "arbitrary")),
        name="out_proj",
    )(a, w, x, mod)


def _swiglu_step(h, wg_ref, wu_ref, wd_ref):
    a = jnp.dot(h, wg_ref[...], preferred_element_type=F32)
    b = jnp.dot(h, wu_ref[...], preferred_element_type=F32)
    return jnp.dot((_silu(a) * b).astype(BF16), wd_ref[...], preferred_element_type=F32)


def _dense_ffn_kernel(x_ref, g_ref, sc_ref, sh_ref, gt_ref, wg_ref, wu_ref, wd_ref, o_ref, h_ref):
    j = pl.program_id(1)

    @pl.when(j == 0)
    def _():
        h_ref[...] = _norm_mod_value(x_ref[...], g_ref[...], sc_ref[0], sh_ref[0]).astype(BF16)
        o_ref[...] = jnp.zeros(o_ref.shape, F32)

    o_ref[...] += _swiglu_step(h_ref[...], wg_ref, wu_ref, wd_ref)

    @pl.when(j == pl.num_programs(1) - 1)
    def _():
        o_ref[...] = x_ref[...] + gt_ref[0] * o_ref[...]


def _dense_ffn(x, g, mod, base, wg, wu, wd, seq):
    t, d = x.shape
    f = wg.shape[1]
    tpb = seq // TM_FF
    row = pl.BlockSpec((TM_FF, d), lambda i, j: (i, 0))
    return pl.pallas_call(
        _dense_ffn_kernel,
        grid=(t // TM_FF, f // TF_FF),
        in_specs=[row,
                  pl.BlockSpec((1, d), lambda i, j: (0, 0)),
                  _mod_spec(base, 4, tpb), _mod_spec(base, 3, tpb), _mod_spec(base, 5, tpb),
                  pl.BlockSpec((d, TF_FF), lambda i, j: (0, j)),
                  pl.BlockSpec((d, TF_FF), lambda i, j: (0, j)),
                  pl.BlockSpec((TF_FF, d), lambda i, j: (j, 0))],
        out_specs=row,
        out_shape=jax.ShapeDtypeStruct((t, d), F32),
        scratch_shapes=[pltpu.VMEM((TM_FF, d), BF16)],
        compiler_params=_params(("arbitrary", "arbitrary")),
        name="dense_ffn",
    )(x, g.reshape(1, d), mod, mod, mod, wg, wu, wd)


META_E1, META_E2, META_R1, META_R2, META_W1, META_W2 = range(6)


def _router_kernel(x_ref, g_ref, sc_ref, sh_ref, wr_ref, br_ref, h_ref, meta_ref, cnt_ref):
    tm = x_ref.shape[0]

    @pl.when(pl.program_id(0) == 0)
    def _():
        cnt_ref[...] = jnp.zeros(cnt_ref.shape, F32)

    h = _norm_mod_value(x_ref[...], g_ref[...], sc_ref[0], sh_ref[0])
    h_ref[...] = h
    lane = lax.broadcasted_iota(jnp.int32, (tm, V7X_LANES), 1).astype(F32)
    logits = jnp.dot(h.astype(BF16), wr_ref[...], preferred_element_type=F32) + br_ref[...]
    logits = jnp.where(lane < N_EXPERTS, logits, NEG)
    m1 = jnp.max(logits, axis=-1, keepdims=True)
    i1 = jnp.min(jnp.where(logits == m1, lane, float(V7X_LANES)), axis=-1, keepdims=True)
    rest = jnp.where(lane == i1, NEG, logits)
    m2 = jnp.max(rest, axis=-1, keepdims=True)
    i2 = jnp.min(jnp.where(rest == m2, lane, float(V7X_LANES)), axis=-1, keepdims=True)
    e21 = jnp.exp(m2 - m1)
    w1 = 1.0 / (1.0 + e21)
    w2 = e21 * w1
    hot1 = lane == i1
    hot2 = lane == i2
    hot = jnp.where(jnp.logical_or(hot1, hot2), 1.0, 0.0)
    ri = lax.broadcasted_iota(jnp.int32, (tm, tm), 0)
    ci = lax.broadcasted_iota(jnp.int32, (tm, tm), 1)
    earlier = jnp.where(ci < ri, 1.0, 0.0).astype(BF16)
    before = jnp.dot(earlier, hot.astype(BF16), preferred_element_type=F32) + cnt_ref[...]
    r1 = jnp.sum(jnp.where(hot1, before, 0.0), axis=-1, keepdims=True)
    r2 = jnp.sum(jnp.where(hot2, before, 0.0), axis=-1, keepdims=True)
    cnt_ref[...] += jnp.sum(hot, axis=0, keepdims=True)
    meta = jnp.zeros((tm, V7X_LANES), F32)
    for slot, val in ((META_E1, i1), (META_E2, i2), (META_R1, r1), (META_R2, r2),
                      (META_W1, w1), (META_W2, w2)):
        meta = jnp.where(lane == slot, val, meta)
    meta_ref[...] = meta


def _router(x, g, mod, base, w_router, b_router, seq):
    t, d = x.shape
    tpb = seq // TM_NORM
    wr = jnp.zeros((d, V7X_LANES), BF16).at[:, :N_EXPERTS].set(w_router.astype(BF16))
    br = jnp.zeros((1, V7X_LANES), F32).at[0, :N_EXPERTS].set(b_router)
    return pl.pallas_call(
        _router_kernel,
        grid=(t // TM_NORM,),
        in_specs=[pl.BlockSpec((TM_NORM, d), lambda i: (i, 0)),
                  pl.BlockSpec((1, d), lambda i: (0, 0)),
                  _mod_spec(base, 4, tpb), _mod_spec(base, 3, tpb),
                  pl.BlockSpec((d, V7X_LANES), lambda i: (0, 0)),
                  pl.BlockSpec((1, V7X_LANES), lambda i: (0, 0))],
        out_specs=[pl.BlockSpec((TM_NORM, d), lambda i: (i, 0)),
                   pl.BlockSpec((TM_NORM, V7X_LANES), lambda i: (i, 0)),
                   pl.BlockSpec((1, V7X_LANES), lambda i: (0, 0))],
        out_shape=[jax.ShapeDtypeStruct((t, d), F32),
                   jax.ShapeDtypeStruct((t, V7X_LANES), F32),
                   jax.ShapeDtypeStruct((1, V7X_LANES), F32)],
        compiler_params=_params(("arbitrary",)),
        name="router",
    )(x, g.reshape(1, d), mod, mod, wr, br)


def _scatter_kernel(pos_ref, h_ref, xs_in_ref, xs_ref, sem):
    del xs_in_ref
    ts = h_ref.shape[0]
    t_total = pl.num_programs(0) * ts
    base = pl.program_id(0) * ts

    def copy(r, k):
        dst = pos_ref[k * t_total + base + r]
        return pltpu.make_async_copy(h_ref.at[pl.ds(r, 1)], xs_ref.at[pl.ds(dst, 1)], sem)

    def start(r, carry):
        copy(r, 0).start(priority=0)
        copy(r, 1).start(priority=1)
        return carry

    def wait(r, carry):
        copy(r, 0).wait()
        copy(r, 1).wait()
        return carry

    lax.fori_loop(0, ts, start, 0)
    lax.fori_loop(0, ts, wait, 0)


def _scatter_rows(h, pos, rows):
    t, d = h.shape
    return pl.pallas_call(
        _scatter_kernel,
        grid_spec=pltpu.PrefetchScalarGridSpec(
            num_scalar_prefetch=1,
            grid=(t // TS_MOVE,),
            in_specs=[pl.BlockSpec((TS_MOVE, d), lambda i, pos: (i, 0)),
                      pl.BlockSpec(memory_space=pl.ANY)],
            out_specs=pl.BlockSpec(memory_space=pl.ANY),
            scratch_shapes=[pltpu.SemaphoreType.DMA(())]),
        out_shape=jax.ShapeDtypeStruct((rows, d), F32),
        input_output_aliases={2: 0},
        compiler_params=_params(("arbitrary",)),
        name="expert_scatter",
    )(pos, h, jnp.zeros((rows, d), F32))


def _expert_ffn_kernel(te_ref, na_ref, xs_ref, wg_ref, wu_ref, wd_ref, o_ref, h_ref):
    i = pl.program_id(0)
    j = pl.program_id(1)
    active = i < na_ref[0]

    @pl.when(j == 0)
    def _():
        h_ref[...] = xs_ref[...].astype(BF16)
        o_ref[...] = jnp.zeros(o_ref.shape, F32)

    @pl.when(active)
    def _():
        o_ref[...] += _swiglu_step(h_ref[...], wg_ref.at[0], wu_ref.at[0], wd_ref.at[0])


def _expert_ffn(xs, tile_expert, n_active, wg, wu, wd):
    rows, d = xs.shape
    f = wg.shape[2]
    nf = f // TF_EXP

    def jcol(i, j, na):
        return jnp.where(i < na[0], j, nf - 1)

    row = pl.BlockSpec((TM_EXP, d), lambda i, j, te, na: (i, 0))
    return pl.pallas_call(
        _expert_ffn_kernel,
        grid_spec=pltpu.PrefetchScalarGridSpec(
            num_scalar_prefetch=2,
            grid=(rows // TM_EXP, nf),
            in_specs=[row,
                      pl.BlockSpec((1, d, TF_EXP), lambda i, j, te, na: (te[i], 0, jcol(i, j, na))),
                      pl.BlockSpec((1, d, TF_EXP), lambda i, j, te, na: (te[i], 0, jcol(i, j, na))),
                      pl.BlockSpec((1, TF_EXP, d), lambda i, j, te, na: (te[i], jcol(i, j, na), 0))],
            out_specs=row,
            scratch_shapes=[pltpu.VMEM((TM_EXP, d), BF16)]),
        out_shape=jax.ShapeDtypeStruct((rows, d), F32),
        compiler_params=_params(("arbitrary", "arbitrary")),
        name="expert_ffn",
    )(tile_expert, n_active, xs, wg, wu, wd)


def _combine_kernel(pos_ref, x_ref, meta_ref, gt_ref, gf_ref, ys_ref, o_ref, y1_ref, y2_ref, sem, *, final_norm):
    ts = x_ref.shape[0]
    t_total = pl.num_programs(0) * ts
    base = pl.program_id(0) * ts

    def copy(r, k):
        src = pos_ref[k * t_total + base + r]
        dst = y1_ref if k == 0 else y2_ref
        return pltpu.make_async_copy(ys_ref.at[pl.ds(src, 1)], dst.at[pl.ds(r, 1)], sem)

    def start(r, carry):
        copy(r, 0).start(priority=0)
        copy(r, 1).start(priority=1)
        return carry

    def wait(r, carry):
        copy(r, 0).wait()
        copy(r, 1).wait()
        return carry

    lax.fori_loop(0, ts, start, 0)
    lax.fori_loop(0, ts, wait, 0)
    meta = meta_ref[...]
    w1 = meta[:, META_W1:META_W1 + 1]
    w2 = meta[:, META_W2:META_W2 + 1]
    f = w1 * y1_ref[...] + w2 * y2_ref[...]
    xn = x_ref[...] + gt_ref[0] * f
    if final_norm:
        xn = xn * lax.rsqrt(jnp.mean(xn * xn, axis=-1, keepdims=True) + EPS) * gf_ref[...]
    o_ref[...] = xn


def _combine(x, meta, pos, ys, mod, base, g_final, seq, final_norm):
    t, d = x.shape
    tpb = seq // TS_MOVE
    row = pl.BlockSpec((TS_MOVE, d), lambda i, pos: (i, 0))
    return pl.pallas_call(
        functools.partial(_combine_kernel, final_norm=final_norm),
        grid_spec=pltpu.PrefetchScalarGridSpec(
            num_scalar_prefetch=1,
            grid=(t // TS_MOVE,),
            in_specs=[row,
                      pl.BlockSpec((TS_MOVE, V7X_LANES), lambda i, pos: (i, 0)),
                      _mod_spec(base, 5, tpb),
                      pl.BlockSpec((1, d), lambda i, pos: (0, 0)),
                      pl.BlockSpec(memory_space=pl.ANY)],
            out_specs=row,
            scratch_shapes=[pltpu.VMEM((TS_MOVE, d), F32), pltpu.VMEM((TS_MOVE, d), F32),
                            pltpu.SemaphoreType.DMA(())]),
        out_shape=jax.ShapeDtypeStruct((t, d), F32),
        compiler_params=_params(("arbitrary",)),
        name="expert_combine",
    )(pos, x, meta, mod, g_final.reshape(1, d), ys)


def _moe_layer(x, g, mod, base, w_router, b_router, wg, wu, wd, g_final, seq, final_norm):
    t, d = x.shape
    h, meta, counts = _router(x, g, mod, base, w_router, b_router, seq)
    counts = counts[0, :N_EXPERTS].astype(jnp.int32)
    tiles = (counts + TM_EXP - 1) // TM_EXP
    tile_end = jnp.cumsum(tiles)
    offsets = (tile_end - tiles) * TM_EXP
    n_tiles = (TOP_K * t) // TM_EXP + N_EXPERTS
    n_active = tile_end[-1:]
    tile_ids = jnp.minimum(jnp.arange(n_tiles, dtype=jnp.int32), n_active[0] - 1)
    tile_expert = jnp.sum(tile_ids[:, None] >= tile_end[None, :], axis=1).astype(jnp.int32)
    e1 = meta[:, META_E1].astype(jnp.int32)
    e2 = meta[:, META_E2].astype(jnp.int32)
    pos = jnp.concatenate([offsets[e1] + meta[:, META_R1].astype(jnp.int32),
                           offsets[e2] + meta[:, META_R2].astype(jnp.int32)])
    xs = _scatter_rows(h, pos, n_tiles * TM_EXP)
    ys = _expert_ffn(xs, tile_expert, n_active.astype(jnp.int32), wg, wu, wd)
    return _combine(x, meta, pos, ys, mod, base, g_final, seq, final_norm)


def _final_norm_kernel(x_ref, g_ref, o_ref):
    x = x_ref[...]
    o_ref[...] = x * lax.rsqrt(jnp.mean(x * x, axis=-1, keepdims=True) + EPS) * g_ref[...]


def _final_norm(x, g):
    t, d = x.shape
    row = pl.BlockSpec((TM_NORM, d), lambda i: (i, 0))
    return pl.pallas_call(
        _final_norm_kernel,
        grid=(t // TM_NORM,),
        in_specs=[row, pl.BlockSpec((1, d), lambda i: (0, 0))],
        out_specs=row,
        out_shape=jax.ShapeDtypeStruct((t, d), F32),
        compiler_params=_params(("arbitrary",)),
        name="final_norm",
    )(x, g.reshape(1, d))


def kernel(x, c, w_ada, b_ada, g_mix, w_in, conv_w, conv_b, conv_ln_g, conv_ln_b, w_proj_a, w_proj_b, w_proj_c,
           w_out, g_ffn, w_ff_gate, w_ff_up, w_ff_down, w_router, b_router, w_exp_gate, w_exp_up, w_exp_down,
           g_final):
    batch, seq, d = x.shape
    depth = w_ada.shape[0]
    assert d == D_MODEL and seq % TM_MM == 0
    xt = x.reshape(batch * seq, d)
    mod = _ada(c, w_ada, b_ada)
    normed = False
    for l in range(depth):
        base = l * batch * 6
        h = _norm_mod(xt, g_mix[l], mod, base, 0, 1, seq)
        u = _in_proj(h, w_in, l, False, BF16)
        att = _in_proj(h, w_in, l, True, F32)
        ca = _conv_branch(u, conv_w[l], conv_b[l], conv_ln_g[l], conv_ln_b[l], batch, seq)
        ob = _attn_branch(att, batch, seq)
        rc = _retention_branch(u, batch, seq)
        merged = _merge(ca, ob, rc, w_proj_a[l].astype(BF16), w_proj_b[l].astype(BF16),
                        w_proj_c[l].astype(BF16), u)
        xt = _matmul_residual(merged, w_out[l].astype(BF16), xt, mod, base, 2, seq)
        j = l // 2
        if l % 2 == 0:
            xt = _dense_ffn(xt, g_ffn[l], mod, base, w_ff_gate[j].astype(BF16), w_ff_up[j].astype(BF16),
                            w_ff_down[j].astype(BF16), seq)
        else:
            normed = l == depth - 1
            xt = _moe_layer(xt, g_ffn[l], mod, base, w_router[j], b_router[j], w_exp_gate[j].astype(BF16),
                            w_exp_up[j].astype(BF16), w_exp_down[j].astype(BF16), g_final, seq, normed)
    if not normed:
        xt = _final_norm(xt, g_final)
    return xt.reshape(batch, seq, d)
```

```python
import functools
import math

import numpy as np
import jax
import jax.numpy as jnp
from jax import lax
from jax.experimental import pallas as pl
from jax.experimental.pallas import tpu as pltpu

F32 = jnp.float32
BF16 = jnp.bfloat16

D_MODEL = 2048
CONV_DIM = 1536
CONV_WIDTH = 31
ATT_HEAD_DIM = 128
ATT_HEADS_PER_GROUP = 4
ATT_PATTERNS = ((128, 1), (512, 4), (2048, 16))
ATT_GROUPS = len(ATT_PATTERNS)
ATT_HEADS = ATT_GROUPS * ATT_HEADS_PER_GROUP
ATT_OUT_DIM = ATT_HEADS_PER_GROUP * ATT_HEAD_DIM
ATT_BLOCK = 128
RET_HEADS = 8
RET_QK_DIM = 128
RET_V_DIM = 256
RET_CHUNK = 128
N_BRANCHES = 3
N_EXPERTS = 8
TOP_K = 2
EPS = 1e-6
COLS_A = 2 * CONV_DIM
COLS_B = 3 * ATT_HEADS * ATT_HEAD_DIM
COLS_QK = RET_HEADS * RET_QK_DIM
COLS_V = RET_HEADS * RET_V_DIM
COLS_GATE = N_BRANCHES * D_MODEL
IN_COLS = COLS_A + COLS_B + 2 * COLS_QK + 2 * COLS_V + COLS_GATE
W_OFF_B = COLS_A
OFF_Q = COLS_A
OFF_K = OFF_Q + COLS_QK
OFF_V = OFF_K + COLS_QK
OFF_G = OFF_V + COLS_V
OFF_GATES = OFF_G + COLS_V
U_COLS = OFF_GATES + COLS_GATE

V7X_VMEM_BYTES = 64 * 1024 * 1024
V7X_LANES = 128
V7X_SUBLANES = 8
VMEM_LIMIT = V7X_VMEM_BYTES * 7 // 8

COL_BLK = 512
TM_NORM = 512
TM_MM = 1024
TM_IN = 2048
TN_IN = 768
ATT_SPAN = ATT_BLOCK * max(d for _, d in ATT_PATTERNS)
TM_FF = 1024
TF_FF = 512
TF_EXP = 1024
TM_EXP = 512
TS_CONV = 256
CONV_HALO = 32
CONV_ROWS = 32
CONV_LANES = 256
TS_MOVE = 512
ROW_COPY_UNROLL = 8
NEG = -1e30


def _params(sem):
    return pltpu.CompilerParams(dimension_semantics=sem, vmem_limit_bytes=VMEM_LIMIT)


def _silu(v):
    return v * jax.nn.sigmoid(v)


def _ada_kernel(c_ref, w_ref, b_ref, o_ref):
    cond = _silu(c_ref[...])
    o_ref[0] = jnp.dot(cond.astype(BF16), w_ref[0].astype(BF16), preferred_element_type=F32) + b_ref[0]


def _ada(c, w_ada, b_ada):
    depth, d, n = w_ada.shape
    b = c.shape[0]
    assert b <= V7X_SUBLANES
    c8 = jnp.zeros((V7X_SUBLANES, d), F32).at[:b].set(c)
    tn = 1536
    out = pl.pallas_call(
        _ada_kernel,
        grid=(depth, n // tn),
        in_specs=[pl.BlockSpec((V7X_SUBLANES, d), lambda l, j: (0, 0)),
                  pl.BlockSpec((1, d, tn), lambda l, j: (l, 0, j)),
                  pl.BlockSpec((1, 1, tn), lambda l, j: (l, 0, j))],
        out_specs=pl.BlockSpec((1, V7X_SUBLANES, tn), lambda l, j: (l, 0, j)),
        out_shape=jax.ShapeDtypeStruct((depth, V7X_SUBLANES, n), F32),
        compiler_params=_params(("arbitrary", "arbitrary")),
        name="ada",
    )(c8, w_ada, b_ada.reshape(depth, 1, n))
    return out[:, :b, :].reshape(depth * b * 6, 1, d)


def _mod_spec(base, which, tiles_per_batch, width=D_MODEL):
    return pl.BlockSpec((1, 1, width), lambda i, *_: (base + (i // tiles_per_batch) * 6 + which, 0, 0))


def _norm_mod_value(x, g, sc, sh):
    ms = jnp.mean(x * x, axis=-1, keepdims=True)
    return (x * lax.rsqrt(ms + EPS) * g) * (1.0 + sc) + sh


def _norm_mod_kernel(x_ref, g_ref, sc_ref, sh_ref, o_ref):
    o_ref[...] = _norm_mod_value(x_ref[...], g_ref[...], sc_ref[0], sh_ref[0]).astype(o_ref.dtype)


def _norm_mod(x, g, mod, base, which_shift, which_scale, seq):
    t, d = x.shape
    tpb = seq // TM_NORM
    return pl.pallas_call(
        _norm_mod_kernel,
        grid=(t // TM_NORM,),
        in_specs=[pl.BlockSpec((TM_NORM, d), lambda i: (i, 0)),
                  pl.BlockSpec((1, d), lambda i: (0, 0)),
                  _mod_spec(base, which_scale, tpb),
                  _mod_spec(base, which_shift, tpb)],
        out_specs=pl.BlockSpec((TM_NORM, d), lambda i: (i, 0)),
        out_shape=jax.ShapeDtypeStruct((t, d), BF16),
        compiler_params=_params(("arbitrary",)),
        name="norm_mod",
    )(x, g.reshape(1, d), mod, mod)


def _in_proj_kernel(a_ref, w_ref, o_ref, wb_ref):
    @pl.when(pl.program_id(1) == 0)
    def _():
        wb_ref[...] = w_ref[...].astype(BF16)

    o_ref[...] = jnp.dot(a_ref[...], wb_ref[...], preferred_element_type=F32).astype(o_ref.dtype)


def _in_proj(a, w_in, layer, attention, out_dtype):
    m, k = a.shape
    first, count = W_OFF_B // TN_IN, COLS_B // TN_IN
    if attention:
        n, wcol = COLS_B, lambda j: first + j
    else:
        n, wcol = U_COLS, lambda j: jnp.where(j < first, j, j + count)
    return pl.pallas_call(
        _in_proj_kernel,
        grid=(n // TN_IN, m // TM_IN),
        in_specs=[pl.BlockSpec((TM_IN, k), lambda j, i: (i, 0)),
                  pl.BlockSpec((None, k, TN_IN), lambda j, i: (layer, 0, wcol(j)))],
        out_specs=pl.BlockSpec((TM_IN, TN_IN), lambda j, i: (i, j)),
        out_shape=jax.ShapeDtypeStruct((m, n), out_dtype),
        scratch_shapes=[pltpu.VMEM((k, TN_IN), BF16)],
        compiler_params=_params(("arbitrary", "arbitrary")),
        name="in_proj_att" if attention else "in_proj",
    )(a, w_in)


def _conv_kernel(a_ref, b_ref, w_ref, cb_ref, lg_ref, lb_ref, o_ref, zs_ref, y_ref):
    ts = a_ref.shape[0]
    c = a_ref.shape[1]

    @pl.when(pl.program_id(1) == 0)
    def _():
        zs_ref[0, 0:CONV_HALO, :] = jnp.zeros((CONV_HALO, c), F32)

    zs_ref[0, CONV_HALO:CONV_HALO + ts, :] = a_ref[...].astype(F32) * jax.nn.sigmoid(b_ref[...].astype(F32))
    span = ts + CONV_HALO - V7X_SUBLANES
    for r in range(1, V7X_SUBLANES):
        zs_ref[r, 0:span, :] = zs_ref[0, r:r + span, :]

    first = CONV_HALO - (CONV_WIDTH - 1)

    groups = CONV_ROWS // V7X_SUBLANES

    def rows(i, carry):
        r0 = pl.multiple_of(i * CONV_ROWS, CONV_ROWS)
        for lc in range(c // CONV_LANES):
            ls = slice(lc * CONV_LANES, (lc + 1) * CONV_LANES)
            acc = jnp.broadcast_to(cb_ref[:, ls][None], (groups, V7X_SUBLANES, CONV_LANES))
            for k in range(CONV_WIDTH):
                off = first + k
                z = zs_ref[off % V7X_SUBLANES, pl.ds(r0 + (off // V7X_SUBLANES) * V7X_SUBLANES, CONV_ROWS), ls]
                acc = acc + w_ref[k, :, ls][None] * z.reshape(groups, V7X_SUBLANES, CONV_LANES)
            y_ref[pl.ds(r0, CONV_ROWS), ls] = acc.reshape(CONV_ROWS, CONV_LANES)
        return carry

    lax.fori_loop(0, ts // CONV_ROWS, rows, 0)

    zs_ref[0, 0:CONV_HALO, :] = zs_ref[0, ts:ts + CONV_HALO, :]

    y = y_ref[...]
    mu = jnp.mean(y, axis=-1, keepdims=True)
    yc = y - mu
    var = jnp.mean(yc * yc, axis=-1, keepdims=True)
    zn = yc * lax.rsqrt(var + EPS) * lg_ref[...] + lb_ref[...]
    o_ref[...] = _silu(zn).astype(o_ref.dtype)


def _conv_branch(u, conv_w, conv_b, ln_g, ln_b, batch, seq):
    t = u.shape[0]
    c = CONV_DIM
    spb = seq // TS_CONV
    w = jnp.broadcast_to(conv_w[:, None, :], (CONV_WIDTH, V7X_SUBLANES, c))
    cb = jnp.broadcast_to(conv_b[None, :], (V7X_SUBLANES, c))
    return pl.pallas_call(
        _conv_kernel,
        grid=(batch, spb),
        in_specs=[pl.BlockSpec((TS_CONV, c), lambda b, s: (b * spb + s, 0)),
                  pl.BlockSpec((TS_CONV, c), lambda b, s: (b * spb + s, 1)),
                  pl.BlockSpec((CONV_WIDTH, V7X_SUBLANES, c), lambda b, s: (0, 0, 0)),
                  pl.BlockSpec((V7X_SUBLANES, c), lambda b, s: (0, 0)),
                  pl.BlockSpec((1, c), lambda b, s: (0, 0)),
                  pl.BlockSpec((1, c), lambda b, s: (0, 0))],
        out_specs=pl.BlockSpec((TS_CONV, c), lambda b, s: (b * spb + s, 0)),
        out_shape=jax.ShapeDtypeStruct((t, c), BF16),
        scratch_shapes=[pltpu.VMEM((V7X_SUBLANES, TS_CONV + CONV_HALO, c), F32),
                        pltpu.VMEM((TS_CONV, c), F32)],
        compiler_params=_params(("arbitrary", "arbitrary")),
        name="conv_branch",
    )(u, u, w, cb, ln_g.reshape(1, c), ln_b.reshape(1, c))


def _rows(start, dilation):
    return pl.ds(start, ATT_BLOCK) if dilation == 1 else pl.ds(start, ATT_BLOCK, stride=dilation)


def _attn_kernel(slope_ref, *refs):
    in_refs, o_ref, og_ref, lg_ref = refs[:5 * ATT_GROUPS], refs[5 * ATT_GROUPS], refs[-2], refs[-1]
    span_idx = pl.program_id(1)
    head = pl.program_id(2)
    blk = ATT_BLOCK
    qi = lax.broadcasted_iota(jnp.int32, (blk, 2 * blk), 0)
    ki = lax.broadcasted_iota(jnp.int32, (blk, 2 * blk), 1)
    step = qi + blk - ki
    stepf = step.astype(F32)
    scale = ATT_HEAD_DIM ** -0.5
    for gi, (_, dilation) in enumerate(ATT_PATTERNS):
        q_ref, kc_ref, kp_ref, vc_ref, vp_ref = in_refs[5 * gi:5 * gi + 5]
        per_span = ATT_SPAN // (blk * dilation)
        slope = slope_ref[gi * ATT_HEADS_PER_GROUP + head] * dilation
        first_limit = jnp.where(ki >= blk, blk, jnp.where(span_idx > 0, blk, 0))
        bias_inner = jnp.where(jnp.logical_and(step >= 0, step <= blk), -slope * stepf, NEG)
        bias_first = jnp.where(jnp.logical_and(step >= 0, step <= first_limit), -slope * stepf, NEG)
        for rho in range(dilation):
            for n in range(per_span):
                cur = _rows(n * blk * dilation + rho, dilation)
                if n > 0:
                    prev = _rows((n - 1) * blk * dilation + rho, dilation)
                    k_prev, v_prev, bias = kc_ref[prev, :], vc_ref[prev, :], bias_inner
                else:
                    prev = _rows(rho, dilation)
                    k_prev, v_prev, bias = kp_ref[prev, :], vp_ref[prev, :], bias_first
                q = q_ref[cur, :].astype(BF16)
                k = jnp.concatenate([k_prev, kc_ref[cur, :]], axis=0).astype(BF16)
                v = jnp.concatenate([v_prev, vc_ref[cur, :]], axis=0).astype(BF16)
                s = lax.dot_general(q, k, (((1,), (1,)), ((), ())), preferred_element_type=F32) * scale + bias
                m = jnp.max(s, axis=-1, keepdims=True)
                p = jnp.exp(s - m)
                denom = jnp.sum(p, axis=-1, keepdims=True)
                og_ref[gi, cur, :] = jnp.dot(p.astype(BF16), v, preferred_element_type=F32) / denom
                lg_ref[gi, cur, :] = jnp.broadcast_to(m + jnp.log(denom), (blk, ATT_HEAD_DIM))
    lses = [lg_ref[gi] for gi in range(ATT_GROUPS)]
    top = functools.reduce(jnp.maximum, lses)
    es = [jnp.exp(l - top) for l in lses]
    num = functools.reduce(jnp.add, [e * og_ref[gi] for gi, e in enumerate(es)])
    o_ref[...] = (num / functools.reduce(jnp.add, es)).astype(o_ref.dtype)


def _attn_branch(att, batch, seq):
    t = att.shape[0]
    assert seq % ATT_SPAN == 0
    spans = seq // ATT_SPAN
    hd = ATT_HEAD_DIM
    slopes = jnp.asarray(2.0 ** (-8.0 * np.arange(1, ATT_HEADS + 1) / ATT_HEADS), F32)
    in_specs = []
    for gi, (window, dilation) in enumerate(ATT_PATTERNS):
        assert window // dilation == ATT_BLOCK and ATT_SPAN % (ATT_BLOCK * dilation) == 0
        reach = ATT_BLOCK * dilation
        per_span = ATT_SPAN // reach

        def cur(which, gi=gi):
            col = (gi * 3 + which) * ATT_HEADS_PER_GROUP
            return pl.BlockSpec((ATT_SPAN, hd), lambda b, s, h, sl: (b * spans + s, col + h))

        def prev(which, gi=gi, per_span=per_span, reach=reach):
            col = (gi * 3 + which) * ATT_HEADS_PER_GROUP
            return pl.BlockSpec((reach, hd),
                                lambda b, s, h, sl: (jnp.maximum((b * spans + s) * per_span - 1, 0), col + h))

        in_specs += [cur(0), cur(1), prev(1), cur(2), prev(2)]
    return pl.pallas_call(
        _attn_kernel,
        grid_spec=pltpu.PrefetchScalarGridSpec(
            num_scalar_prefetch=1,
            grid=(batch, spans, ATT_HEADS_PER_GROUP),
            in_specs=in_specs,
            out_specs=pl.BlockSpec((ATT_SPAN, hd), lambda b, s, h, sl: (b * spans + s, h)),
            scratch_shapes=[pltpu.VMEM((ATT_GROUPS, ATT_SPAN, hd), F32),
                            pltpu.VMEM((ATT_GROUPS, ATT_SPAN, hd), F32)]),
        out_shape=jax.ShapeDtypeStruct((t, ATT_OUT_DIM), BF16),
        compiler_params=_params(("arbitrary", "arbitrary", "arbitrary")),
        name="attention",
    )(slopes, *([att] * (5 * ATT_GROUPS)))


RET_HEADS_PER_STEP = 8


def _retention_tables():
    h = np.arange(RET_HEADS, dtype=np.float64)
    log_gamma = np.log1p(-(2.0 ** (-5.0 - h)))
    idx = np.arange(RET_CHUNK, dtype=np.float64)
    diff = idx[:, None] - idx[None, :]
    scale = RET_QK_DIM ** -0.5
    intra = np.where(diff >= 0, np.exp(log_gamma[:, None, None] * np.maximum(diff, 0.0)), 0.0) * scale
    q_decay = np.exp(log_gamma[:, None] * (idx + 1.0))
    k_decay = np.exp(log_gamma[:, None] * (RET_CHUNK - 1.0 - idx)) * scale
    chunk_decay = np.exp(log_gamma * RET_CHUNK)
    qd = np.broadcast_to(q_decay[:, :, None], (RET_HEADS, RET_CHUNK, RET_QK_DIM))
    kd = np.broadcast_to(k_decay[:, :, None], (RET_HEADS, RET_CHUNK, RET_QK_DIM))
    cd = np.broadcast_to(chunk_decay[:, None, None], (RET_HEADS, 1, RET_V_DIM))
    return tuple(jnp.asarray(a, F32) for a in (intra, qd, kd, cd))


def _retention_kernel(q_ref, k_ref, va_ref, vb_ref, ga_ref, gb_ref, intra_ref, qd_ref, kd_ref, cd_ref,
                      o_ref, st_ref):
    @pl.when(pl.program_id(2) == 0)
    def _():
        st_ref[...] = jnp.zeros(st_ref.shape, F32)

    half = RET_HEADS_PER_STEP // 2
    for hh in range(RET_HEADS_PER_STEP):
        qs = slice(hh * RET_QK_DIM, (hh + 1) * RET_QK_DIM)
        vs = slice((hh % half) * RET_V_DIM, (hh % half + 1) * RET_V_DIM)
        q = q_ref[:, qs]
        k = k_ref[:, qs]
        v = (va_ref if hh < half else vb_ref)[:, vs]
        g = (ga_ref if hh < half else gb_ref)[:, vs].astype(F32)
        s = lax.dot_general(q, k, (((1,), (1,)), ((), ())), preferred_element_type=F32) * intra_ref[hh]
        q_dec = (q.astype(F32) * qd_ref[hh]).astype(BF16)
        st = st_ref[hh]
        o = (jnp.dot(s.astype(BF16), v, preferred_element_type=F32)
             + jnp.dot(q_dec, st.astype(BF16), preferred_element_type=F32))
        k_dec = (k.astype(F32) * kd_ref[hh]).T.astype(BF16)
        st_ref[hh] = st * cd_ref[hh] + jnp.dot(k_dec, v, preferred_element_type=F32)
        on = o * lax.rsqrt(jnp.mean(o * o, axis=-1, keepdims=True) + EPS)
        o_ref[:, hh * RET_V_DIM:(hh + 1) * RET_V_DIM] = (_silu(g) * on).astype(o_ref.dtype)


def _retention_branch(u, batch, seq):
    t = u.shape[0]
    hps = RET_HEADS_PER_STEP
    blk = hps * RET_QK_DIM
    assert (hps // 2) * RET_V_DIM == blk and all(off % blk == 0 for off in (OFF_Q, OFF_K, OFF_V, OFF_G))
    nchunk = seq // RET_CHUNK
    groups = RET_HEADS // hps

    def col(off, mult, add):
        return pl.BlockSpec((RET_CHUNK, blk), lambda b, hg, c: (b * nchunk + c, off // blk + mult * hg + add))

    def table(shape):
        return pl.BlockSpec((hps,) + shape, lambda b, hg, c: (hg, 0, 0))

    return pl.pallas_call(
        _retention_kernel,
        grid=(batch, groups, nchunk),
        in_specs=[col(OFF_Q, 1, 0), col(OFF_K, 1, 0), col(OFF_V, 2, 0), col(OFF_V, 2, 1),
                  col(OFF_G, 2, 0), col(OFF_G, 2, 1),
                  table((RET_CHUNK, RET_CHUNK)), table((RET_CHUNK, RET_QK_DIM)), table((RET_CHUNK, RET_QK_DIM)),
                  table((1, RET_V_DIM))],
        out_specs=pl.BlockSpec((RET_CHUNK, hps * RET_V_DIM), lambda b, hg, c: (b * nchunk + c, hg)),
        out_shape=jax.ShapeDtypeStruct((t, COLS_V), BF16),
        scratch_shapes=[pltpu.VMEM((hps, RET_QK_DIM, RET_V_DIM), F32)],
        compiler_params=_params(("arbitrary", "arbitrary", "arbitrary")),
        name="retention",
    )(u, u, u, u, u, u, *_retention_tables())


def _merge_kernel(ca_ref, ob_ref, rc_ref, wa_ref, wb_ref, wc_ref, g0_ref, g1_ref, g2_ref, o_ref):
    acc = None
    for x_ref, w_ref, g_ref in ((ca_ref, wa_ref, g0_ref), (ob_ref, wb_ref, g1_ref), (rc_ref, wc_ref, g2_ref)):
        y = jnp.dot(x_ref[...], w_ref[...], preferred_element_type=F32)
        term = jax.nn.sigmoid(g_ref[...].astype(F32)) * y
        acc = term if acc is None else acc + term
    o_ref[...] = acc.astype(o_ref.dtype)


def _merge(ca, ob, rc, wa, wb, wc, u):
    t = ca.shape[0]
    d = wa.shape[1]
    tn = COL_BLK
    gate_blk = OFF_GATES // tn
    per_gate = d // tn

    def lhs(a):
        return pl.BlockSpec((TM_MM, a.shape[1]), lambda i, j: (i, 0))

    def rhs(w):
        return pl.BlockSpec((w.shape[0], tn), lambda i, j: (0, j))

    def gate(k):
        return pl.BlockSpec((TM_MM, tn), lambda i, j: (i, gate_blk + k * per_gate + j))

    return pl.pallas_call(
        _merge_kernel,
        grid=(t // TM_MM, d // tn),
        in_specs=[lhs(ca), lhs(ob), lhs(rc), rhs(wa), rhs(wb), rhs(wc), gate(0), gate(1), gate(2)],
        out_specs=pl.BlockSpec((TM_MM, tn), lambda i, j: (i, j)),
        out_shape=jax.ShapeDtypeStruct((t, d), BF16),
        compiler_params=_params(("arbitrary", "arbitrary")),
        name="merge",
    )(ca, ob, rc, wa, wb, wc, u, u, u)


def _matmul_residual_kernel(a_ref, w_ref, x_ref, gt_ref, o_ref):
    y = jnp.dot(a_ref[...], w_ref[...], preferred_element_type=F32)
    o_ref[...] = x_ref[...] + gt_ref[0] * y


def _matmul_residual(a, w, x, mod, base, which, seq):
    t, k = a.shape
    d = w.shape[1]
    tn = COL_BLK
    tpb = seq // TM_MM
    return pl.pallas_call(
        _matmul_residual_kernel,
        grid=(t // TM_MM, d // tn),
        in_specs=[pl.BlockSpec((TM_MM, k), lambda i, j: (i, 0)),
                  pl.BlockSpec((k, tn), lambda i, j: (0, j)),
                  pl.BlockSpec((TM_MM, tn), lambda i, j: (i, j)),
                  pl.BlockSpec((1, 1, tn), lambda i, j: (base + (i // tpb) * 6 + which, 0, j))],
        out_specs=pl.BlockSpec((TM_MM, tn), lambda i, j: (i, j)),
        out_shape=jax.ShapeDtypeStruct((t, d), F32),
        compiler_params=_params(("arbitrary", "arbitrary")),
        name="out_proj",
    )(a, w, x, mod)


def _swiglu_step(h, wg_ref, wu_ref, wd_ref):
    a = jnp.dot(h, wg_ref[...], preferred_element_type=F32)
    b = jnp.dot(h, wu_ref[...], preferred_element_type=F32)
    return jnp.dot((_silu(a) * b).astype(BF16), wd_ref[...], preferred_element_type=F32)


def _dense_ffn_kernel(x_ref, g_ref, sc_ref, sh_ref, gt_ref, wg_ref, wu_ref, wd_ref, o_ref, h_ref):
    j = pl.program_id(1)

    @pl.when(j == 0)
    def _():
        h_ref[...] = _norm_mod_value(x_ref[...], g_ref[...], sc_ref[0], sh_ref[0]).astype(BF16)
        o_ref[...] = jnp.zeros(o_ref.shape, F32)

    o_ref[...] += _swiglu_step(h_ref[...], wg_ref, wu_ref, wd_ref)

    @pl.when(j == pl.num_programs(1) - 1)
    def _():
        o_ref[...] = x_ref[...] + gt_ref[0] * o_ref[...]


def _dense_ffn(x, g, mod, base, wg, wu, wd, seq):
    t, d = x.shape
    f = wg.shape[1]
    tpb = seq // TM_FF
    row = pl.BlockSpec((TM_FF, d), lambda i, j: (i, 0))
    x_row = pl.BlockSpec((TM_FF, d), lambda i, j: (i, 0), pipeline_mode=pl.Buffered(1))
    return pl.pallas_call(
        _dense_ffn_kernel,
        grid=(t // TM_FF, f // TF_FF),
        in_specs=[x_row,
                  pl.BlockSpec((1, d), lambda i, j: (0, 0)),
                  _mod_spec(base, 4, tpb), _mod_spec(base, 3, tpb), _mod_spec(base, 5, tpb),
                  pl.BlockSpec((d, TF_FF), lambda i, j: (0, j)),
                  pl.BlockSpec((d, TF_FF), lambda i, j: (0, j)),
                  pl.BlockSpec((TF_FF, d), lambda i, j: (j, 0))],
        out_specs=row,
        out_shape=jax.ShapeDtypeStruct((t, d), F32),
        scratch_shapes=[pltpu.VMEM((TM_FF, d), BF16)],
        compiler_params=_params(("arbitrary", "arbitrary")),
        name="dense_ffn",
    )(x, g.reshape(1, d), mod, mod, mod, wg, wu, wd)


META_E1, META_E2, META_R1, META_R2, META_W1, META_W2 = range(6)


def _router_kernel(x_ref, g_ref, sc_ref, sh_ref, wr_ref, br_ref, h_ref, meta_ref, cnt_ref):
    tm = x_ref.shape[0]

    @pl.when(pl.program_id(0) == 0)
    def _():
        cnt_ref[...] = jnp.zeros(cnt_ref.shape, F32)

    h = _norm_mod_value(x_ref[...], g_ref[...], sc_ref[0], sh_ref[0])
    h_ref[...] = h
    lane = lax.broadcasted_iota(jnp.int32, (tm, V7X_LANES), 1).astype(F32)
    logits = jnp.dot(h.astype(BF16), wr_ref[...], preferred_element_type=F32) + br_ref[...]
    logits = jnp.where(lane < N_EXPERTS, logits, NEG)
    m1 = jnp.max(logits, axis=-1, keepdims=True)
    i1 = jnp.min(jnp.where(logits == m1, lane, float(V7X_LANES)), axis=-1, keepdims=True)
    rest = jnp.where(lane == i1, NEG, logits)
    m2 = jnp.max(rest, axis=-1, keepdims=True)
    i2 = jnp.min(jnp.where(rest == m2, lane, float(V7X_LANES)), axis=-1, keepdims=True)
    e21 = jnp.exp(m2 - m1)
    w1 = 1.0 / (1.0 + e21)
    w2 = e21 * w1
    hot1 = lane == i1
    hot2 = lane == i2
    hot = jnp.where(jnp.logical_or(hot1, hot2), 1.0, 0.0)
    ri = lax.broadcasted_iota(jnp.int32, (tm, tm), 0)
    ci = lax.broadcasted_iota(jnp.int32, (tm, tm), 1)
    earlier = jnp.where(ci < ri, 1.0, 0.0).astype(BF16)
    before = jnp.dot(earlier, hot.astype(BF16), preferred_element_type=F32) + cnt_ref[...]
    r1 = jnp.sum(jnp.where(hot1, before, 0.0), axis=-1, keepdims=True)
    r2 = jnp.sum(jnp.where(hot2, before, 0.0), axis=-1, keepdims=True)
    cnt_ref[...] += jnp.sum(hot, axis=0, keepdims=True)
    meta = jnp.zeros((tm, V7X_LANES), F32)
    for slot, val in ((META_E1, i1), (META_E2, i2), (META_R1, r1), (META_R2, r2),
                      (META_W1, w1), (META_W2, w2)):
        meta = jnp.where(lane == slot, val, meta)
    meta_ref[...] = meta


def _router(x, g, mod, base, w_router, b_router, seq):
    t, d = x.shape
    tpb = seq // TM_NORM
    wr = jnp.zeros((d, V7X_LANES), BF16).at[:, :N_EXPERTS].set(w_router.astype(BF16))
    br = jnp.zeros((1, V7X_LANES), F32).at[0, :N_EXPERTS].set(b_router)
    return pl.pallas_call(
        _router_kernel,
        grid=(t // TM_NORM,),
        in_specs=[pl.BlockSpec((TM_NORM, d), lambda i: (i, 0)),
                  pl.BlockSpec((1, d), lambda i: (0, 0)),
                  _mod_spec(base, 4, tpb), _mod_spec(base, 3, tpb),
                  pl.BlockSpec((d, V7X_LANES), lambda i: (0, 0)),
                  pl.BlockSpec((1, V7X_LANES), lambda i: (0, 0))],
        out_specs=[pl.BlockSpec((TM_NORM, d), lambda i: (i, 0)),
                   pl.BlockSpec((TM_NORM, V7X_LANES), lambda i: (i, 0)),
                   pl.BlockSpec((1, V7X_LANES), lambda i: (0, 0))],
        out_shape=[jax.ShapeDtypeStruct((t, d), F32),
                   jax.ShapeDtypeStruct((t, V7X_LANES), F32),
                   jax.ShapeDtypeStruct((1, V7X_LANES), F32)],
        compiler_params=_params(("arbitrary",)),
        name="router",
    )(x, g.reshape(1, d), mod, mod, wr, br)


def _scatter_kernel(pos_ref, h_ref, xs_in_ref, xs_ref, sem):
    del xs_in_ref
    ts = h_ref.shape[0]
    t_total = pl.num_programs(0) * ts
    base = pl.program_id(0) * ts

    def copy(r, k):
        dst = pos_ref[k * t_total + base + r]
        return pltpu.make_async_copy(h_ref.at[pl.ds(r, 1)], xs_ref.at[pl.ds(dst, 1)], sem)

    def start(r, carry):
        copy(r, 0).start()
        copy(r, 1).start()
        return carry

    lax.fori_loop(0, ts, start, 0, unroll=ROW_COPY_UNROLL)
    for _ in range(TOP_K):
        pltpu.make_async_copy(h_ref, xs_ref.at[pl.ds(0, ts)], sem).wait()


def _scatter_rows(h, pos, rows):
    t, d = h.shape
    return pl.pallas_call(
        _scatter_kernel,
        grid_spec=pltpu.PrefetchScalarGridSpec(
            num_scalar_prefetch=1,
            grid=(t // TS_MOVE,),
            in_specs=[pl.BlockSpec((TS_MOVE, d), lambda i, pos: (i, 0)),
                      pl.BlockSpec(memory_space=pl.ANY)],
            out_specs=pl.BlockSpec(memory_space=pl.ANY),
            scratch_shapes=[pltpu.SemaphoreType.DMA(())]),
        out_shape=jax.ShapeDtypeStruct((rows, d), F32),
        input_output_aliases={2: 0},
        compiler_params=_params(("arbitrary",)),
        name="expert_scatter",
    )(pos, h, jnp.zeros((rows, d), F32))


def _expert_ffn_kernel(te_ref, na_ref, xs_ref, wg_ref, wu_ref, wd_ref, o_ref, h_ref):
    i = pl.program_id(0)
    j = pl.program_id(1)
    active = i < na_ref[0]

    @pl.when(j == 0)
    def _():
        h_ref[...] = xs_ref[...].astype(BF16)
        o_ref[...] = jnp.zeros(o_ref.shape, F32)

    @pl.when(active)
    def _():
        o_ref[...] += _swiglu_step(h_ref[...], wg_ref.at[0], wu_ref.at[0], wd_ref.at[0])


def _expert_ffn(xs, tile_expert, n_active, wg, wu, wd):
    rows, d = xs.shape
    f = wg.shape[2]
    nf = f // TF_EXP

    def jcol(i, j, na):
        return jnp.where(i < na[0], j, nf - 1)

    row = pl.BlockSpec((TM_EXP, d), lambda i, j, te, na: (i, 0))
    return pl.pallas_call(
        _expert_ffn_kernel,
        grid_spec=pltpu.PrefetchScalarGridSpec(
            num_scalar_prefetch=2,
            grid=(rows // TM_EXP, nf),
            in_specs=[row,
                      pl.BlockSpec((1, d, TF_EXP), lambda i, j, te, na: (te[i], 0, jcol(i, j, na))),
                      pl.BlockSpec((1, d, TF_EXP), lambda i, j, te, na: (te[i], 0, jcol(i, j, na))),
                      pl.BlockSpec((1, TF_EXP, d), lambda i, j, te, na: (te[i], jcol(i, j, na), 0))],
            out_specs=row,
            scratch_shapes=[pltpu.VMEM((TM_EXP, d), BF16)]),
        out_shape=jax.ShapeDtypeStruct((rows, d), F32),
        compiler_params=_params(("arbitrary", "arbitrary")),
        name="expert_ffn",
    )(tile_expert, n_active, xs, wg, wu, wd)


def _combine_kernel(pos_ref, x_ref, meta_ref, gt_ref, gf_ref, ys_ref, o_ref, y1_ref, y2_ref, sem, *, final_norm):
    ts = x_ref.shape[0]
    t_total = pl.num_programs(0) * ts
    base = pl.program_id(0) * ts

    def copy(r, k):
        src = pos_ref[k * t_total + base + r]
        dst = y1_ref if k == 0 else y2_ref
        return pltpu.make_async_copy(ys_ref.at[pl.ds(src, 1)], dst.at[pl.ds(r, 1)], sem)

    def start(r, carry):
        copy(r, 0).start()
        copy(r, 1).start()
        return carry

    lax.fori_loop(0, ts, start, 0, unroll=ROW_COPY_UNROLL)
    for buf in (y1_ref, y2_ref):
        pltpu.make_async_copy(ys_ref.at[pl.ds(0, ts)], buf, sem).wait()
    meta = meta_ref[...]
    w1 = meta[:, META_W1:META_W1 + 1]
    w2 = meta[:, META_W2:META_W2 + 1]
    f = w1 * y1_ref[...] + w2 * y2_ref[...]
    xn = x_ref[...] + gt_ref[0] * f
    if final_norm:
        xn = xn * lax.rsqrt(jnp.mean(xn * xn, axis=-1, keepdims=True) + EPS) * gf_ref[...]
    o_ref[...] = xn


def _combine(x, meta, pos, ys, mod, base, g_final, seq, final_norm):
    t, d = x.shape
    tpb = seq // TS_MOVE
    row = pl.BlockSpec((TS_MOVE, d), lambda i, pos: (i, 0))
    return pl.pallas_call(
        functools.partial(_combine_kernel, final_norm=final_norm),
        grid_spec=pltpu.PrefetchScalarGridSpec(
            num_scalar_prefetch=1,
            grid=(t // TS_MOVE,),
            in_specs=[row,
                      pl.BlockSpec((TS_MOVE, V7X_LANES), lambda i, pos: (i, 0)),
                      _mod_spec(base, 5, tpb),
                      pl.BlockSpec((1, d), lambda i, pos: (0, 0)),
                      pl.BlockSpec(memory_space=pl.ANY)],
            out_specs=row,
            scratch_shapes=[pltpu.VMEM((TS_MOVE, d), F32), pltpu.VMEM((TS_MOVE, d), F32),
                            pltpu.SemaphoreType.DMA(())]),
        out_shape=jax.ShapeDtypeStruct((t, d), F32),
        compiler_params=_params(("arbitrary",)),
        name="expert_combine",
    )(pos, x, meta, mod, g_final.reshape(1, d), ys)


def _moe_layer(x, g, mod, base, w_router, b_router, wg, wu, wd, g_final, seq, final_norm):
    t, d = x.shape
    h, meta, counts = _router(x, g, mod, base, w_router, b_router, seq)
    counts = counts[0, :N_EXPERTS].astype(jnp.int32)
    tiles = (counts + TM_EXP - 1) // TM_EXP
    tile_end = jnp.cumsum(tiles)
    offsets = (tile_end - tiles) * TM_EXP
    n_tiles = (TOP_K * t) // TM_EXP + N_EXPERTS
    n_active = tile_end[-1:]
    tile_ids = jnp.minimum(jnp.arange(n_tiles, dtype=jnp.int32), n_active[0] - 1)
    tile_expert = jnp.sum(tile_ids[:, None] >= tile_end[None, :], axis=1).astype(jnp.int32)
    e1 = meta[:, META_E1].astype(jnp.int32)
    e2 = meta[:, META_E2].astype(jnp.int32)
    pos = jnp.concatenate([offsets[e1] + meta[:, META_R1].astype(jnp.int32),
                           offsets[e2] + meta[:, META_R2].astype(jnp.int32)])
    xs = _scatter_rows(h, pos, n_tiles * TM_EXP)
    ys = _expert_ffn(xs, tile_expert, n_active.astype(jnp.int32), wg, wu, wd)
    return _combine(x, meta, pos, ys, mod, base, g_final, seq, final_norm)


def _final_norm_kernel(x_ref, g_ref, o_ref):
    x = x_ref[...]
    o_ref[...] = x * lax.rsqrt(jnp.mean(x * x, axis=-1, keepdims=True) + EPS) * g_ref[...]


def _final_norm(x, g):
    t, d = x.shape
    row = pl.BlockSpec((TM_NORM, d), lambda i: (i, 0))
    return pl.pallas_call(
        _final_norm_kernel,
        grid=(t // TM_NORM,),
        in_specs=[row, pl.BlockSpec((1, d), lambda i: (0, 0))],
        out_specs=row,
        out_shape=jax.ShapeDtypeStruct((t, d), F32),
        compiler_params=_params(("arbitrary",)),
        name="final_norm",
    )(x, g.reshape(1, d))


def kernel(x, c, w_ada, b_ada, g_mix, w_in, conv_w, conv_b, conv_ln_g, conv_ln_b, w_proj_a, w_proj_b, w_proj_c,
           w_out, g_ffn, w_ff_gate, w_ff_up, w_ff_down, w_router, b_router, w_exp_gate, w_exp_up, w_exp_down,
           g_final):
    batch, seq, d = x.shape
    depth = w_ada.shape[0]
    assert d == D_MODEL and seq % TM_MM == 0
    xt = x.reshape(batch * seq, d)
    mod = _ada(c, w_ada, b_ada)
    normed = False
    for l in range(depth):
        base = l * batch * 6
        h = _norm_mod(xt, g_mix[l], mod, base, 0, 1, seq)
        u = _in_proj(h, w_in, l, False, BF16)
        att = _in_proj(h, w_in, l, True, F32)
        ca = _conv_branch(u, conv_w[l], conv_b[l], conv_ln_g[l], conv_ln_b[l], batch, seq)
        ob = _attn_branch(att, batch, seq)
        rc = _retention_branch(u, batch, seq)
        merged = _merge(ca, ob, rc, w_proj_a[l].astype(BF16), w_proj_b[l].astype(BF16),
                        w_proj_c[l].astype(BF16), u)
        xt = _matmul_residual(merged, w_out[l].astype(BF16), xt, mod, base, 2, seq)
        j = l // 2
        if l % 2 == 0:
            xt = _dense_ffn(xt, g_ffn[l], mod, base, w_ff_gate[j].astype(BF16), w_ff_up[j].astype(BF16),
                            w_ff_down[j].astype(BF16), seq)
        else:
            normed = l == depth - 1
            xt = _moe_layer(xt, g_ffn[l], mod, base, w_router[j], b_router[j], w_exp_gate[j].astype(BF16),
                            w_exp_up[j].astype(BF16), w_exp_down[j].astype(BF16), g_final, seq, normed)
    if not normed:
        xt = _final_norm(xt, g_final)
    return xt.reshape(batch, seq, d)
```

```python
import functools

import numpy as np
import jax
import jax.numpy as jnp
from jax import lax
from jax.experimental import pallas as pl
from jax.experimental.pallas import tpu as pltpu

F32 = jnp.float32
BF16 = jnp.bfloat16

D_MODEL = 2048
CONV_DIM = 1536
CONV_WIDTH = 31
ATT_HEAD_DIM = 128
ATT_HEADS_PER_GROUP = 4
ATT_PATTERNS = ((128, 1), (512, 4), (2048, 16))
ATT_GROUPS = len(ATT_PATTERNS)
ATT_HEADS = ATT_GROUPS * ATT_HEADS_PER_GROUP
ATT_OUT_DIM = ATT_HEADS_PER_GROUP * ATT_HEAD_DIM
ATT_BLOCK = 128
RET_HEADS = 8
RET_QK_DIM = 128
RET_V_DIM = 256
RET_CHUNK = 128
N_BRANCHES = 3
N_EXPERTS = 8
TOP_K = 2
EPS = 1e-6
COLS_A = 2 * CONV_DIM
COLS_B = 3 * ATT_HEADS * ATT_HEAD_DIM
COLS_QK = RET_HEADS * RET_QK_DIM
COLS_V = RET_HEADS * RET_V_DIM
COLS_GATE = N_BRANCHES * D_MODEL
IN_COLS = COLS_A + COLS_B + 2 * COLS_QK + 2 * COLS_V + COLS_GATE
W_OFF_B = COLS_A
OFF_Q = COLS_A
OFF_K = OFF_Q + COLS_QK
OFF_V = OFF_K + COLS_QK
OFF_G = OFF_V + COLS_V
OFF_GATES = OFF_G + COLS_V
U_COLS = OFF_GATES + COLS_GATE

V7X_VMEM_BYTES = 64 * 1024 * 1024
V7X_LANES = 128
V7X_SUBLANES = 8
VMEM_LIMIT = V7X_VMEM_BYTES * 7 // 8

COL_BLK = 512
TM_NORM = 512
TM_MM = 1024
TM_IN = 2048
TN_OUT = 1024
TN_IN = 768
ATT_SPAN = ATT_BLOCK * max(d for _, d in ATT_PATTERNS)
TM_FF = 512
TF_FF = 512
TF_EXP = 1024
TM_EXP = 512
TS_CONV = 256
CONV_HALO = 32
CONV_ROWS = 64
CONV_LANES = 128
TS_MOVE = 512
ROW_COPY_UNROLL = 16
NEG = -1e30


def _params(sem):
    return pltpu.CompilerParams(dimension_semantics=sem, vmem_limit_bytes=VMEM_LIMIT)


def _silu(v):
    return v * jax.nn.sigmoid(v)


def _ada_kernel(c_ref, w_ref, b_ref, o_ref):
    cond = _silu(c_ref[...])
    o_ref[0] = jnp.dot(cond.astype(BF16), w_ref[0].astype(BF16), preferred_element_type=F32) + b_ref[0]


def _ada(c, w_ada, b_ada):
    depth, d, n = w_ada.shape
    b = c.shape[0]
    assert b <= V7X_SUBLANES
    c8 = jnp.zeros((V7X_SUBLANES, d), F32).at[:b].set(c)
    tn = 1536
    out = pl.pallas_call(
        _ada_kernel,
        grid=(depth, n // tn),
        in_specs=[pl.BlockSpec((V7X_SUBLANES, d), lambda l, j: (0, 0)),
                  pl.BlockSpec((1, d, tn), lambda l, j: (l, 0, j)),
                  pl.BlockSpec((1, 1, tn), lambda l, j: (l, 0, j))],
        out_specs=pl.BlockSpec((1, V7X_SUBLANES, tn), lambda l, j: (l, 0, j)),
        out_shape=jax.ShapeDtypeStruct((depth, V7X_SUBLANES, n), F32),
        compiler_params=_params(("arbitrary", "arbitrary")),
        name="ada",
    )(c8, w_ada, b_ada.reshape(depth, 1, n))
    return out[:, :b, :].reshape(depth * b * 6, 1, d)


def _mod_spec(base, which, tiles_per_batch, width=D_MODEL):
    return pl.BlockSpec((1, 1, width), lambda i, *_: (base + (i // tiles_per_batch) * 6 + which, 0, 0))


def _norm_mod_value(x, g, sc, sh):
    ms = jnp.mean(x * x, axis=-1, keepdims=True)
    return (x * lax.rsqrt(ms + EPS) * g) * (1.0 + sc) + sh


def _norm_mod_kernel(x_ref, g_ref, sc_ref, sh_ref, o_ref):
    o_ref[...] = _norm_mod_value(x_ref[...], g_ref[...], sc_ref[0], sh_ref[0]).astype(o_ref.dtype)


def _norm_mod(x, g, mod, base, which_shift, which_scale, seq):
    t, d = x.shape
    tpb = seq // TM_NORM
    return pl.pallas_call(
        _norm_mod_kernel,
        grid=(t // TM_NORM,),
        in_specs=[pl.BlockSpec((TM_NORM, d), lambda i: (i, 0)),
                  pl.BlockSpec((1, d), lambda i: (0, 0)),
                  _mod_spec(base, which_scale, tpb),
                  _mod_spec(base, which_shift, tpb)],
        out_specs=pl.BlockSpec((TM_NORM, d), lambda i: (i, 0)),
        out_shape=jax.ShapeDtypeStruct((t, d), BF16),
        compiler_params=_params(("arbitrary",)),
        name="norm_mod",
    )(x, g.reshape(1, d), mod, mod)


def _in_proj_kernel(a_ref, w_ref, o_ref, wb_ref):
    @pl.when(pl.program_id(1) == 0)
    def _():
        wb_ref[...] = w_ref[...].astype(BF16)

    o_ref[...] = jnp.dot(a_ref[...], wb_ref[...], preferred_element_type=F32).astype(o_ref.dtype)


def _in_proj(a, w_in, layer, attention, out_dtype):
    m, k = a.shape
    first, count = W_OFF_B // TN_IN, COLS_B // TN_IN
    if attention:
        n, wcol = COLS_B, lambda j: first + j
    else:
        n, wcol = U_COLS, lambda j: jnp.where(j < first, j, j + count)
    return pl.pallas_call(
        _in_proj_kernel,
        grid=(n // TN_IN, m // TM_IN),
        in_specs=[pl.BlockSpec((TM_IN, k), lambda j, i: (i, 0)),
                  pl.BlockSpec((None, k, TN_IN), lambda j, i: (layer, 0, wcol(j)))],
        out_specs=pl.BlockSpec((TM_IN, TN_IN), lambda j, i: (i, j)),
        out_shape=jax.ShapeDtypeStruct((m, n), out_dtype),
        scratch_shapes=[pltpu.VMEM((k, TN_IN), BF16)],
        compiler_params=_params(("arbitrary", "arbitrary")),
        name="in_proj_att" if attention else "in_proj",
    )(a, w_in)


def _conv_kernel(a_ref, b_ref, w_ref, cb_ref, lg_ref, lb_ref, o_ref, zs_ref, y_ref):
    ts = a_ref.shape[0]
    c = a_ref.shape[1]

    @pl.when(pl.program_id(1) == 0)
    def _():
        zs_ref[0, 0:CONV_HALO, :] = jnp.zeros((CONV_HALO, c), F32)

    zs_ref[0, CONV_HALO:CONV_HALO + ts, :] = a_ref[...].astype(F32) * jax.nn.sigmoid(b_ref[...].astype(F32))
    span = ts + CONV_HALO - V7X_SUBLANES
    for r in range(1, V7X_SUBLANES):
        zs_ref[r, 0:span, :] = zs_ref[0, r:r + span, :]

    first = CONV_HALO - (CONV_WIDTH - 1)

    groups = CONV_ROWS // V7X_SUBLANES

    def rows(i, carry):
        r0 = pl.multiple_of(i * CONV_ROWS, CONV_ROWS)
        for lc in range(c // CONV_LANES):
            ls = slice(lc * CONV_LANES, (lc + 1) * CONV_LANES)
            acc = jnp.broadcast_to(cb_ref[:, ls][None], (groups, V7X_SUBLANES, CONV_LANES))
            for r in range(V7X_SUBLANES):
                taps = [k for k in range(CONV_WIDTH) if (first + k) % V7X_SUBLANES == r]
                lo = min((first + k) // V7X_SUBLANES for k in taps)
                hi = max((first + k) // V7X_SUBLANES for k in taps)
                window = zs_ref[r, pl.ds(r0 + lo * V7X_SUBLANES, CONV_ROWS + (hi - lo) * V7X_SUBLANES), ls]
                window = window.reshape(groups + hi - lo, V7X_SUBLANES, CONV_LANES)
                for k in taps:
                    a = (first + k) // V7X_SUBLANES - lo
                    acc = acc + w_ref[k, :, ls][None] * window[a:a + groups]
            y_ref[pl.ds(r0, CONV_ROWS), ls] = acc.reshape(CONV_ROWS, CONV_LANES)
        return carry

    lax.fori_loop(0, ts // CONV_ROWS, rows, 0)

    zs_ref[0, 0:CONV_HALO, :] = zs_ref[0, ts:ts + CONV_HALO, :]

    y = y_ref[...]
    mu = jnp.mean(y, axis=-1, keepdims=True)
    yc = y - mu
    var = jnp.mean(yc * yc, axis=-1, keepdims=True)
    zn = yc * lax.rsqrt(var + EPS) * lg_ref[...] + lb_ref[...]
    o_ref[...] = _silu(zn).astype(o_ref.dtype)


def _conv_branch(u, conv_w, conv_b, ln_g, ln_b, batch, seq):
    t = u.shape[0]
    c = CONV_DIM
    spb = seq // TS_CONV
    w = jnp.broadcast_to(conv_w[:, None, :], (CONV_WIDTH, V7X_SUBLANES, c))
    cb = jnp.broadcast_to(conv_b[None, :], (V7X_SUBLANES, c))
    return pl.pallas_call(
        _conv_kernel,
        grid=(batch, spb),
        in_specs=[pl.BlockSpec((TS_CONV, c), lambda b, s: (b * spb + s, 0)),
                  pl.BlockSpec((TS_CONV, c), lambda b, s: (b * spb + s, 1)),
                  pl.BlockSpec((CONV_WIDTH, V7X_SUBLANES, c), lambda b, s: (0, 0, 0)),
                  pl.BlockSpec((V7X_SUBLANES, c), lambda b, s: (0, 0)),
                  pl.BlockSpec((1, c), lambda b, s: (0, 0)),
                  pl.BlockSpec((1, c), lambda b, s: (0, 0))],
        out_specs=pl.BlockSpec((TS_CONV, c), lambda b, s: (b * spb + s, 0)),
        out_shape=jax.ShapeDtypeStruct((t, c), BF16),
        scratch_shapes=[pltpu.VMEM((V7X_SUBLANES, TS_CONV + CONV_HALO, c), F32),
                        pltpu.VMEM((TS_CONV, c), F32)],
        compiler_params=_params(("arbitrary", "arbitrary")),
        name="conv_branch",
    )(u, u, w, cb, ln_g.reshape(1, c), ln_b.reshape(1, c))


def _rows(start, dilation):
    return pl.ds(start, ATT_BLOCK) if dilation == 1 else pl.ds(start, ATT_BLOCK, stride=dilation)


def _attn_kernel(slope_ref, *refs):
    in_refs, o_ref, og_ref, lg_ref = refs[:5 * ATT_GROUPS], refs[5 * ATT_GROUPS], refs[-2], refs[-1]
    span_idx = pl.program_id(1)
    head = pl.program_id(2)
    blk = ATT_BLOCK
    qi = lax.broadcasted_iota(jnp.int32, (blk, 2 * blk), 0)
    ki = lax.broadcasted_iota(jnp.int32, (blk, 2 * blk), 1)
    step = qi + blk - ki
    stepf = step.astype(F32)
    scale = ATT_HEAD_DIM ** -0.5
    for gi, (_, dilation) in enumerate(ATT_PATTERNS):
        q_ref, kc_ref, kp_ref, vc_ref, vp_ref = in_refs[5 * gi:5 * gi + 5]
        per_span = ATT_SPAN // (blk * dilation)
        slope = slope_ref[gi * ATT_HEADS_PER_GROUP + head] * dilation
        first_limit = jnp.where(ki >= blk, blk, jnp.where(span_idx > 0, blk, 0))
        bias_inner = jnp.where(jnp.logical_and(step >= 0, step <= blk), -slope * stepf, NEG)
        bias_first = jnp.where(jnp.logical_and(step >= 0, step <= first_limit), -slope * stepf, NEG)
        for rho in range(dilation):
            for n in range(per_span):
                cur = _rows(n * blk * dilation + rho, dilation)
                if n > 0:
                    prev = _rows((n - 1) * blk * dilation + rho, dilation)
                    k_prev, v_prev, bias = kc_ref[prev, :], vc_ref[prev, :], bias_inner
                else:
                    prev = _rows(rho, dilation)
                    k_prev, v_prev, bias = kp_ref[prev, :], vp_ref[prev, :], bias_first
                q = q_ref[cur, :].astype(BF16)
                k = jnp.concatenate([k_prev, kc_ref[cur, :]], axis=0).astype(BF16)
                v = jnp.concatenate([v_prev, vc_ref[cur, :]], axis=0).astype(BF16)
                s = lax.dot_general(q, k, (((1,), (1,)), ((), ())), preferred_element_type=F32) * scale + bias
                m = jnp.max(s, axis=-1, keepdims=True)
                p = jnp.exp(s - m)
                denom = jnp.sum(p, axis=-1, keepdims=True)
                og_ref[gi, cur, :] = jnp.dot(p.astype(BF16), v, preferred_element_type=F32) / denom
                lg_ref[gi, cur, :] = jnp.broadcast_to(m + jnp.log(denom), (blk, ATT_HEAD_DIM))
    lses = [lg_ref[gi] for gi in range(ATT_GROUPS)]
    top = functools.reduce(jnp.maximum, lses)
    es = [jnp.exp(l - top) for l in lses]
    num = functools.reduce(jnp.add, [e * og_ref[gi] for gi, e in enumerate(es)])
    o_ref[...] = (num / functools.reduce(jnp.add, es)).astype(o_ref.dtype)


def _attn_branch(att, batch, seq):
    t = att.shape[0]
    assert seq % ATT_SPAN == 0
    spans = seq // ATT_SPAN
    hd = ATT_HEAD_DIM
    slopes = jnp.asarray(2.0 ** (-8.0 * np.arange(1, ATT_HEADS + 1) / ATT_HEADS), F32)
    in_specs = []
    for gi, (window, dilation) in enumerate(ATT_PATTERNS):
        assert window // dilation == ATT_BLOCK and ATT_SPAN % (ATT_BLOCK * dilation) == 0
        reach = ATT_BLOCK * dilation
        per_span = ATT_SPAN // reach

        def cur(which, gi=gi):
            col = (gi * 3 + which) * ATT_HEADS_PER_GROUP
            return pl.BlockSpec((ATT_SPAN, hd), lambda b, s, h, sl: (b * spans + s, col + h))

        def prev(which, gi=gi, per_span=per_span, reach=reach):
            col = (gi * 3 + which) * ATT_HEADS_PER_GROUP
            return pl.BlockSpec((reach, hd),
                                lambda b, s, h, sl: (jnp.maximum((b * spans + s) * per_span - 1, 0), col + h))

        in_specs += [cur(0), cur(1), prev(1), cur(2), prev(2)]
    return pl.pallas_call(
        _attn_kernel,
        grid_spec=pltpu.PrefetchScalarGridSpec(
            num_scalar_prefetch=1,
            grid=(batch, spans, ATT_HEADS_PER_GROUP),
            in_specs=in_specs,
            out_specs=pl.BlockSpec((ATT_SPAN, hd), lambda b, s, h, sl: (b * spans + s, h)),
            scratch_shapes=[pltpu.VMEM((ATT_GROUPS, ATT_SPAN, hd), F32),
                            pltpu.VMEM((ATT_GROUPS, ATT_SPAN, hd), F32)]),
        out_shape=jax.ShapeDtypeStruct((t, ATT_OUT_DIM), BF16),
        compiler_params=_params(("arbitrary", "arbitrary", "arbitrary")),
        name="attention",
    )(slopes, *([att] * (5 * ATT_GROUPS)))


RET_HEADS_PER_STEP = 8


def _retention_tables():
    h = np.arange(RET_HEADS, dtype=np.float64)
    log_gamma = np.log1p(-(2.0 ** (-5.0 - h)))
    idx = np.arange(RET_CHUNK, dtype=np.float64)
    diff = idx[:, None] - idx[None, :]
    scale = RET_QK_DIM ** -0.5
    intra = np.where(diff >= 0, np.exp(log_gamma[:, None, None] * np.maximum(diff, 0.0)), 0.0) * scale
    q_decay = np.exp(log_gamma[:, None] * (idx + 1.0))
    k_decay = np.exp(log_gamma[:, None] * (RET_CHUNK - 1.0 - idx)) * scale
    chunk_decay = np.exp(log_gamma * RET_CHUNK)
    qd = np.broadcast_to(q_decay[:, :, None], (RET_HEADS, RET_CHUNK, RET_QK_DIM))
    kd = np.broadcast_to(k_decay[:, :, None], (RET_HEADS, RET_CHUNK, RET_QK_DIM))
    cd = np.broadcast_to(chunk_decay[:, None, None], (RET_HEADS, 1, RET_V_DIM))
    return tuple(jnp.asarray(a, F32) for a in (intra, qd, kd, cd))


def _retention_kernel(q_ref, k_ref, va_ref, vb_ref, ga_ref, gb_ref, intra_ref, qd_ref, kd_ref, cd_ref,
                      o_ref, st_ref):
    @pl.when(pl.program_id(2) == 0)
    def _():
        st_ref[...] = jnp.zeros(st_ref.shape, F32)

    half = RET_HEADS_PER_STEP // 2
    for hh in range(RET_HEADS_PER_STEP):
        qs = slice(hh * RET_QK_DIM, (hh + 1) * RET_QK_DIM)
        vs = slice((hh % half) * RET_V_DIM, (hh % half + 1) * RET_V_DIM)
        q = q_ref[:, qs]
        k = k_ref[:, qs]
        v = (va_ref if hh < half else vb_ref)[:, vs]
        g = (ga_ref if hh < half else gb_ref)[:, vs].astype(F32)
        s = lax.dot_general(q, k, (((1,), (1,)), ((), ())), preferred_element_type=F32) * intra_ref[hh]
        q_dec = (q.astype(F32) * qd_ref[hh]).astype(BF16)
        st = st_ref[hh]
        o = (jnp.dot(s.astype(BF16), v, preferred_element_type=F32)
             + jnp.dot(q_dec, st.astype(BF16), preferred_element_type=F32))
        k_dec = (k.astype(F32) * kd_ref[hh]).T.astype(BF16)
        st_ref[hh] = st * cd_ref[hh] + jnp.dot(k_dec, v, preferred_element_type=F32)
        on = o * lax.rsqrt(jnp.mean(o * o, axis=-1, keepdims=True) + EPS)
        o_ref[:, hh * RET_V_DIM:(hh + 1) * RET_V_DIM] = (_silu(g) * on).astype(o_ref.dtype)


def _retention_branch(u, batch, seq):
    t = u.shape[0]
    hps = RET_HEADS_PER_STEP
    blk = hps * RET_QK_DIM
    assert (hps // 2) * RET_V_DIM == blk and all(off % blk == 0 for off in (OFF_Q, OFF_K, OFF_V, OFF_G))
    nchunk = seq // RET_CHUNK
    groups = RET_HEADS // hps

    def col(off, mult, add):
        return pl.BlockSpec((RET_CHUNK, blk), lambda b, hg, c: (b * nchunk + c, off // blk + mult * hg + add))

    def table(shape):
        return pl.BlockSpec((hps,) + shape, lambda b, hg, c: (hg, 0, 0))

    return pl.pallas_call(
        _retention_kernel,
        grid=(batch, groups, nchunk),
        in_specs=[col(OFF_Q, 1, 0), col(OFF_K, 1, 0), col(OFF_V, 2, 0), col(OFF_V, 2, 1),
                  col(OFF_G, 2, 0), col(OFF_G, 2, 1),
                  table((RET_CHUNK, RET_CHUNK)), table((RET_CHUNK, RET_QK_DIM)), table((RET_CHUNK, RET_QK_DIM)),
                  table((1, RET_V_DIM))],
        out_specs=pl.BlockSpec((RET_CHUNK, hps * RET_V_DIM), lambda b, hg, c: (b * nchunk + c, hg)),
        out_shape=jax.ShapeDtypeStruct((t, COLS_V), BF16),
        scratch_shapes=[pltpu.VMEM((hps, RET_QK_DIM, RET_V_DIM), F32)],
        compiler_params=_params(("arbitrary", "arbitrary", "arbitrary")),
        name="retention",
    )(u, u, u, u, u, u, *_retention_tables())


def _merge_kernel(ca_ref, ob_ref, rc_ref, wa_ref, wb_ref, wc_ref, g0_ref, g1_ref, g2_ref, o_ref):
    acc = None
    for x_ref, w_ref, g_ref in ((ca_ref, wa_ref, g0_ref), (ob_ref, wb_ref, g1_ref), (rc_ref, wc_ref, g2_ref)):
        y = jnp.dot(x_ref[...], w_ref[...], preferred_element_type=F32)
        term = jax.nn.sigmoid(g_ref[...].astype(F32)) * y
        acc = term if acc is None else acc + term
    o_ref[...] = acc.astype(o_ref.dtype)


def _merge(ca, ob, rc, wa, wb, wc, u):
    t = ca.shape[0]
    d = wa.shape[1]
    tn = COL_BLK
    gate_blk = OFF_GATES // tn
    per_gate = d // tn

    def lhs(a):
        return pl.BlockSpec((TM_MM, a.shape[1]), lambda i, j: (i, 0))

    def rhs(w):
        return pl.BlockSpec((w.shape[0], tn), lambda i, j: (0, j))

    def gate(k):
        return pl.BlockSpec((TM_MM, tn), lambda i, j: (i, gate_blk + k * per_gate + j))

    return pl.pallas_call(
        _merge_kernel,
        grid=(t // TM_MM, d // tn),
        in_specs=[lhs(ca), lhs(ob), lhs(rc), rhs(wa), rhs(wb), rhs(wc), gate(0), gate(1), gate(2)],
        out_specs=pl.BlockSpec((TM_MM, tn), lambda i, j: (i, j)),
        out_shape=jax.ShapeDtypeStruct((t, d), BF16),
        compiler_params=_params(("arbitrary", "arbitrary")),
        name="merge",
    )(ca, ob, rc, wa, wb, wc, u, u, u)


def _matmul_residual_kernel(a_ref, w_ref, x_ref, gt_ref, o_ref):
    y = jnp.dot(a_ref[...], w_ref[...], preferred_element_type=F32)
    o_ref[...] = x_ref[...] + gt_ref[0] * y


def _matmul_residual(a, w, x, mod, base, which, seq):
    t, k = a.shape
    d = w.shape[1]
    tn = TN_OUT
    tpb = seq // TM_MM
    return pl.pallas_call(
        _matmul_residual_kernel,
        grid=(t // TM_MM, d // tn),
        in_specs=[pl.BlockSpec((TM_MM, k), lambda i, j: (i, 0)),
                  pl.BlockSpec((k, tn), lambda i, j: (0, j)),
                  pl.BlockSpec((TM_MM, tn), lambda i, j: (i, j)),
                  pl.BlockSpec((1, 1, tn), lambda i, j: (base + (i // tpb) * 6 + which, 0, j))],
        out_specs=pl.BlockSpec((TM_MM, tn), lambda i, j: (i, j)),
        out_shape=jax.ShapeDtypeStruct((t, d), F32),
        compiler_params=_params(("arbitrary", "arbitrary")),
        name="out_proj",
    )(a, w, x, mod)


def _swiglu_step(h, wg_ref, wu_ref, wd_ref):
    a = jnp.dot(h, wg_ref[...], preferred_element_type=F32)
    b = jnp.dot(h, wu_ref[...], preferred_element_type=F32)
    return jnp.dot((_silu(a) * b).astype(BF16), wd_ref[...], preferred_element_type=F32)


def _dense_ffn_kernel(x_ref, g_ref, sc_ref, sh_ref, gt_ref, wg_ref, wu_ref, wd_ref, o_ref, h_ref):
    j = pl.program_id(1)

    @pl.when(j == 0)
    def _():
        h_ref[...] = _norm_mod_value(x_ref[...], g_ref[...], sc_ref[0], sh_ref[0]).astype(BF16)
        o_ref[...] = jnp.zeros(o_ref.shape, F32)

    o_ref[...] += _swiglu_step(h_ref[...], wg_ref, wu_ref, wd_ref)

    @pl.when(j == pl.num_programs(1) - 1)
    def _():
        o_ref[...] = x_ref[...] + gt_ref[0] * o_ref[...]


def _dense_ffn(x, g, mod, base, wg, wu, wd, seq):
    t, d = x.shape
    f = wg.shape[1]
    tpb = seq // TM_FF
    row = pl.BlockSpec((TM_FF, d), lambda i, j: (i, 0))
    return pl.pallas_call(
        _dense_ffn_kernel,
        grid=(t // TM_FF, f // TF_FF),
        in_specs=[row,
                  pl.BlockSpec((1, d), lambda i, j: (0, 0)),
                  _mod_spec(base, 4, tpb), _mod_spec(base, 3, tpb), _mod_spec(base, 5, tpb),
                  pl.BlockSpec((d, TF_FF), lambda i, j: (0, j)),
                  pl.BlockSpec((d, TF_FF), lambda i, j: (0, j)),
                  pl.BlockSpec((TF_FF, d), lambda i, j: (j, 0))],
        out_specs=row,
        out_shape=jax.ShapeDtypeStruct((t, d), F32),
        scratch_shapes=[pltpu.VMEM((TM_FF, d), BF16)],
        compiler_params=_params(("arbitrary", "arbitrary")),
        name="dense_ffn",
    )(x, g.reshape(1, d), mod, mod, mod, wg, wu, wd)


META_E1, META_E2, META_R1, META_R2, META_W1, META_W2 = range(6)
N_PAD_RANGES = N_EXPERTS + 1


def _router_kernel(x_ref, g_ref, sc_ref, sh_ref, wr_ref, br_ref, h_ref, meta_ref, cnt_ref):
    tm = x_ref.shape[0]

    @pl.when(pl.program_id(0) == 0)
    def _():
        cnt_ref[...] = jnp.zeros(cnt_ref.shape, F32)

    h = _norm_mod_value(x_ref[...], g_ref[...], sc_ref[0], sh_ref[0])
    h_ref[...] = h
    lane = lax.broadcasted_iota(jnp.int32, (tm, V7X_LANES), 1).astype(F32)
    logits = jnp.dot(h.astype(BF16), wr_ref[...], preferred_element_type=F32) + br_ref[...]
    logits = jnp.where(lane < N_EXPERTS, logits, NEG)
    m1 = jnp.max(logits, axis=-1, keepdims=True)
    i1 = jnp.min(jnp.where(logits == m1, lane, float(V7X_LANES)), axis=-1, keepdims=True)
    rest = jnp.where(lane == i1, NEG, logits)
    m2 = jnp.max(rest, axis=-1, keepdims=True)
    i2 = jnp.min(jnp.where(rest == m2, lane, float(V7X_LANES)), axis=-1, keepdims=True)
    e21 = jnp.exp(m2 - m1)
    w1 = 1.0 / (1.0 + e21)
    w2 = e21 * w1
    hot1 = lane == i1
    hot2 = lane == i2
    hot = jnp.where(jnp.logical_or(hot1, hot2), 1.0, 0.0)
    ri = lax.broadcasted_iota(jnp.int32, (tm, tm), 0)
    ci = lax.broadcasted_iota(jnp.int32, (tm, tm), 1)
    earlier = jnp.where(ci < ri, 1.0, 0.0).astype(BF16)
    before = jnp.dot(earlier, hot.astype(BF16), preferred_element_type=F32) + cnt_ref[...]
    r1 = jnp.sum(jnp.where(hot1, before, 0.0), axis=-1, keepdims=True)
    r2 = jnp.sum(jnp.where(hot2, before, 0.0), axis=-1, keepdims=True)
    cnt_ref[...] += jnp.sum(hot, axis=0, keepdims=True)
    meta = jnp.zeros((tm, V7X_LANES), F32)
    for slot, val in ((META_E1, i1), (META_E2, i2), (META_R1, r1), (META_R2, r2),
                      (META_W1, w1), (META_W2, w2)):
        meta = jnp.where(lane == slot, val, meta)
    meta_ref[...] = meta


def _router(x, g, mod, base, w_router, b_router, seq):
    t, d = x.shape
    tpb = seq // TM_NORM
    wr = jnp.zeros((d, V7X_LANES), BF16).at[:, :N_EXPERTS].set(w_router.astype(BF16))
    br = jnp.zeros((1, V7X_LANES), F32).at[0, :N_EXPERTS].set(b_router)
    return pl.pallas_call(
        _router_kernel,
        grid=(t // TM_NORM,),
        in_specs=[pl.BlockSpec((TM_NORM, d), lambda i: (i, 0)),
                  pl.BlockSpec((1, d), lambda i: (0, 0)),
                  _mod_spec(base, 4, tpb), _mod_spec(base, 3, tpb),
                  pl.BlockSpec((d, V7X_LANES), lambda i: (0, 0)),
                  pl.BlockSpec((1, V7X_LANES), lambda i: (0, 0))],
        out_specs=[pl.BlockSpec((TM_NORM, d), lambda i: (i, 0)),
                   pl.BlockSpec((TM_NORM, V7X_LANES), lambda i: (i, 0)),
                   pl.BlockSpec((1, V7X_LANES), lambda i: (0, 0))],
        out_shape=[jax.ShapeDtypeStruct((t, d), F32),
                   jax.ShapeDtypeStruct((t, V7X_LANES), F32),
                   jax.ShapeDtypeStruct((1, V7X_LANES), F32)],
        compiler_params=_params(("arbitrary",)),
        name="router",
    )(x, g.reshape(1, d), mod, mod, wr, br)


def _scatter_kernel(pos_ref, pads_ref, h_ref, xs_ref, zrow_ref, sem, zsem):
    @pl.when(pl.program_id(0) == 0)
    def _():
        zrow_ref[...] = jnp.zeros(zrow_ref.shape, F32)

        def zero_row(dst):
            return pltpu.make_async_copy(zrow_ref.at[pl.ds(0, 1)], xs_ref.at[pl.ds(dst, 1)], zsem)

        for e in range(N_PAD_RANGES):
            first, count = pads_ref[e], pads_ref[N_PAD_RANGES + e]

            def start_zero(r, carry):
                zero_row(first + r).start()
                return carry

            lax.fori_loop(0, count, start_zero, 0)
        for e in range(N_PAD_RANGES):
            first, count = pads_ref[e], pads_ref[N_PAD_RANGES + e]

            def wait_zero(r, carry):
                zero_row(first + r).wait()
                return carry

            lax.fori_loop(0, count, wait_zero, 0)

    ts = h_ref.shape[0]
    t_total = pl.num_programs(0) * ts
    base = pl.program_id(0) * ts

    def copy(r, k):
        dst = pos_ref[k * t_total + base + r]
        return pltpu.make_async_copy(h_ref.at[pl.ds(r, 1)], xs_ref.at[pl.ds(dst, 1)], sem)

    def start(r, carry):
        copy(r, 0).start()
        copy(r, 1).start()
        return carry

    lax.fori_loop(0, ts, start, 0, unroll=ROW_COPY_UNROLL)
    for _ in range(TOP_K):
        pltpu.make_async_copy(h_ref, xs_ref.at[pl.ds(0, ts)], sem).wait()


def _scatter_rows(h, pos, pads, rows):
    t, d = h.shape
    return pl.pallas_call(
        _scatter_kernel,
        grid_spec=pltpu.PrefetchScalarGridSpec(
            num_scalar_prefetch=2,
            grid=(t // TS_MOVE,),
            in_specs=[pl.BlockSpec((TS_MOVE, d), lambda i, pos, pads: (i, 0))],
            out_specs=pl.BlockSpec(memory_space=pl.ANY),
            scratch_shapes=[pltpu.VMEM((V7X_SUBLANES, d), F32),
                            pltpu.SemaphoreType.DMA(()), pltpu.SemaphoreType.DMA(())]),
        out_shape=jax.ShapeDtypeStruct((rows, d), F32),
        compiler_params=_params(("arbitrary",)),
        name="expert_scatter",
    )(pos, pads, h)


def _expert_ffn_kernel(te_ref, na_ref, xs_ref, wg_ref, wu_ref, wd_ref, o_ref, h_ref):
    i = pl.program_id(0)
    j = pl.program_id(1)
    active = i < na_ref[0]

    @pl.when(j == 0)
    def _():
        o_ref[...] = jnp.zeros(o_ref.shape, F32)

    @pl.when(jnp.logical_and(active, j == 0))
    def _():
        h_ref[...] = xs_ref[...].astype(BF16)

    @pl.when(active)
    def _():
        o_ref[...] += _swiglu_step(h_ref[...], wg_ref.at[0], wu_ref.at[0], wd_ref.at[0])


def _expert_ffn(xs, tile_expert, n_active, wg, wu, wd):
    rows, d = xs.shape
    f = wg.shape[2]
    nf = f // TF_EXP

    def jcol(i, j, na):
        return jnp.where(i < na[0], j, nf - 1)

    row = pl.BlockSpec((TM_EXP, d), lambda i, j, te, na: (i, 0))
    in_row = pl.BlockSpec((TM_EXP, d), lambda i, j, te, na: (jnp.minimum(i, na[0] - 1), 0))
    return pl.pallas_call(
        _expert_ffn_kernel,
        grid_spec=pltpu.PrefetchScalarGridSpec(
            num_scalar_prefetch=2,
            grid=(rows // TM_EXP, nf),
            in_specs=[in_row,
                      pl.BlockSpec((1, d, TF_EXP), lambda i, j, te, na: (te[i], 0, jcol(i, j, na))),
                      pl.BlockSpec((1, d, TF_EXP), lambda i, j, te, na: (te[i], 0, jcol(i, j, na))),
                      pl.BlockSpec((1, TF_EXP, d), lambda i, j, te, na: (te[i], jcol(i, j, na), 0))],
            out_specs=row,
            scratch_shapes=[pltpu.VMEM((TM_EXP, d), BF16)]),
        out_shape=jax.ShapeDtypeStruct((rows, d), F32),
        compiler_params=_params(("arbitrary", "arbitrary")),
        name="expert_ffn",
    )(tile_expert, n_active, xs, wg, wu, wd)


def _combine_kernel(pos_ref, x_ref, meta_ref, gt_ref, gf_ref, ys_ref, o_ref, y1_ref, y2_ref, sem, *, final_norm):
    ts = x_ref.shape[0]
    t_total = pl.num_programs(0) * ts
    base = pl.program_id(0) * ts

    def copy(r, k):
        src = pos_ref[k * t_total + base + r]
        dst = y1_ref if k == 0 else y2_ref
        return pltpu.make_async_copy(ys_ref.at[pl.ds(src, 1)], dst.at[pl.ds(r, 1)], sem)

    def start(r, carry):
        copy(r, 0).start()
        copy(r, 1).start()
        return carry

    lax.fori_loop(0, ts, start, 0, unroll=ROW_COPY_UNROLL)
    for buf in (y1_ref, y2_ref):
        pltpu.make_async_copy(ys_ref.at[pl.ds(0, ts)], buf, sem).wait()
    meta = meta_ref[...]
    w1 = meta[:, META_W1:META_W1 + 1]
    w2 = meta[:, META_W2:META_W2 + 1]
    f = w1 * y1_ref[...] + w2 * y2_ref[...]
    xn = x_ref[...] + gt_ref[0] * f
    if final_norm:
        xn = xn * lax.rsqrt(jnp.mean(xn * xn, axis=-1, keepdims=True) + EPS) * gf_ref[...]
    o_ref[...] = xn


def _combine(x, meta, pos, ys, mod, base, g_final, seq, final_norm):
    t, d = x.shape
    tpb = seq // TS_MOVE
    row = pl.BlockSpec((TS_MOVE, d), lambda i, pos: (i, 0))
    return pl.pallas_call(
        functools.partial(_combine_kernel, final_norm=final_norm),
        grid_spec=pltpu.PrefetchScalarGridSpec(
            num_scalar_prefetch=1,
            grid=(t // TS_MOVE,),
            in_specs=[row,
                      pl.BlockSpec((TS_MOVE, V7X_LANES), lambda i, pos: (i, 0)),
                      _mod_spec(base, 5, tpb),
                      pl.BlockSpec((1, d), lambda i, pos: (0, 0)),
                      pl.BlockSpec(memory_space=pl.ANY)],
            out_specs=row,
            scratch_shapes=[pltpu.VMEM((TS_MOVE, d), F32), pltpu.VMEM((TS_MOVE, d), F32),
                            pltpu.SemaphoreType.DMA(())]),
        out_shape=jax.ShapeDtypeStruct((t, d), F32),
        compiler_params=_params(("arbitrary",)),
        name="expert_combine",
    )(pos, x, meta, mod, g_final.reshape(1, d), ys)


def _moe_layer(x, g, mod, base, w_router, b_router, wg, wu, wd, g_final, seq, final_norm):
    t, d = x.shape
    h, meta, counts = _router(x, g, mod, base, w_router, b_router, seq)
    counts = counts[0, :N_EXPERTS].astype(jnp.int32)
    tiles = (counts + TM_EXP - 1) // TM_EXP
    tile_end = jnp.cumsum(tiles)
    offsets = (tile_end - tiles) * TM_EXP
    n_tiles = (TOP_K * t) // TM_EXP + N_EXPERTS
    n_active = tile_end[-1:]
    tile_ids = jnp.minimum(jnp.arange(n_tiles, dtype=jnp.int32), n_active[0] - 1)
    tile_expert = jnp.sum(tile_ids[:, None] >= tile_end[None, :], axis=1).astype(jnp.int32)
    e1 = meta[:, META_E1].astype(jnp.int32)
    e2 = meta[:, META_E2].astype(jnp.int32)
    pos = jnp.concatenate([offsets[e1] + meta[:, META_R1].astype(jnp.int32),
                           offsets[e2] + meta[:, META_R2].astype(jnp.int32)])
    used = tile_end[-1:] * TM_EXP
    pads = jnp.concatenate([offsets + counts, used,
                            tiles * TM_EXP - counts, n_tiles * TM_EXP - used]).astype(jnp.int32)
    xs = _scatter_rows(h, pos, pads, n_tiles * TM_EXP)
    ys = _expert_ffn(xs, tile_expert, n_active.astype(jnp.int32), wg, wu, wd)
    return _combine(x, meta, pos, ys, mod, base, g_final, seq, final_norm)


def _final_norm_kernel(x_ref, g_ref, o_ref):
    x = x_ref[...]
    o_ref[...] = x * lax.rsqrt(jnp.mean(x * x, axis=-1, keepdims=True) + EPS) * g_ref[...]


def _final_norm(x, g):
    t, d = x.shape
    row = pl.BlockSpec((TM_NORM, d), lambda i: (i, 0))
    return pl.pallas_call(
        _final_norm_kernel,
        grid=(t // TM_NORM,),
        in_specs=[row, pl.BlockSpec((1, d), lambda i: (0, 0))],
        out_specs=row,
        out_shape=jax.ShapeDtypeStruct((t, d), F32),
        compiler_params=_params(("arbitrary",)),
        name="final_norm",
    )(x, g.reshape(1, d))


def kernel(x, c, w_ada, b_ada, g_mix, w_in, conv_w, conv_b, conv_ln_g, conv_ln_b, w_proj_a, w_proj_b, w_proj_c,
           w_out, g_ffn, w_ff_gate, w_ff_up, w_ff_down, w_router, b_router, w_exp_gate, w_exp_up, w_exp_down,
           g_final):
    batch, seq, d = x.shape
    depth = w_ada.shape[0]
    assert d == D_MODEL and seq % TM_MM == 0
    xt = x.reshape(batch * seq, d)
    mod = _ada(c, w_ada, b_ada)
    normed = False
    for l in range(depth):
        base = l * batch * 6
        h = _norm_mod(xt, g_mix[l], mod, base, 0, 1, seq)
        u = _in_proj(h, w_in, l, False, BF16)
        att = _in_proj(h, w_in, l, True, F32)
        ca = _conv_branch(u, conv_w[l], conv_b[l], conv_ln_g[l], conv_ln_b[l], batch, seq)
        ob = _attn_branch(att, batch, seq)
        rc = _retention_branch(u, batch, seq)
        merged = _merge(ca, ob, rc, w_proj_a[l].astype(BF16), w_proj_b[l].astype(BF16),
                        w_proj_c[l].astype(BF16), u)
        xt = _matmul_residual(merged, w_out[l].astype(BF16), xt, mod, base, 2, seq)
        j = l // 2
        if l % 2 == 0:
            xt = _dense_ffn(xt, g_ffn[l], mod, base, w_ff_gate[j].astype(BF16), w_ff_up[j].astype(BF16),
                            w_ff_down[j].astype(BF16), seq)
        else:
            normed = l == depth - 1
            xt = _moe_layer(xt, g_ffn[l], mod, base, w_router[j], b_router[j], w_exp_gate[j].astype(BF16),
                            w_exp_up[j].astype(BF16), w_exp_down[j].astype(BF16), g_final, seq, normed)
    if not normed:
        xt = _final_norm(xt, g_final)
    return xt.reshape(batch, seq, d)
```
